```python
import jax, jax.numpy as jnp
from jax import lax
import numpy as np

D_MODEL = 1024
BATCH = 32
SEQ = 2048
DEPTH = 2
DEC_BATCH = 2
DEC_SEQ = 8192
PAST_LEN = 128

N_HEADS = 8
QK_NOPE_DIM = 64
QK_ROPE_DIM = 64
V_HEAD_DIM = 64
Q_LORA_RANK = 384
KV_LORA_RANK = 256
ATTN_WIDTH = N_HEADS * V_HEAD_DIM
ROPE_BASE = 10000.0
Q_BLOCK = 128
CONV_WIDTH = 512
CONV_KERNEL = 31
D_FF = 4 * D_MODEL
NORM_EPS = 1e-6
N_MOD = 6

COL_SPLITS = (
    2 * CONV_WIDTH,
    2 * CONV_WIDTH + Q_LORA_RANK,
    2 * CONV_WIDTH + Q_LORA_RANK + KV_LORA_RANK,
    2 * CONV_WIDTH + Q_LORA_RANK + KV_LORA_RANK + QK_ROPE_DIM,
    2 * CONV_WIDTH + Q_LORA_RANK + KV_LORA_RANK + QK_ROPE_DIM + D_MODEL,
)
IN_COLS = 2 * CONV_WIDTH + Q_LORA_RANK + KV_LORA_RANK + QK_ROPE_DIM + 2 * D_MODEL

kernel_name = "hybrid_conformer_mla_encoder"


def rms_norm(x, g):
    x32 = x.astype(jnp.float32)
    y = x32 * lax.rsqrt(jnp.mean(x32 * x32, axis=-1, keepdims=True) + NORM_EPS)
    return (y * g.astype(jnp.float32)).astype(x.dtype)


def layer_norm(x, g, b):
    x32 = x.astype(jnp.float32)
    mu = jnp.mean(x32, axis=-1, keepdims=True)
    xc = x32 - mu
    y = xc * lax.rsqrt(jnp.mean(xc * xc, axis=-1, keepdims=True) + NORM_EPS)
    return (y * g.astype(jnp.float32) + b.astype(jnp.float32)).astype(x.dtype)


def rope_tables(seq_len):
    inv = ROPE_BASE ** (-jnp.arange(0, QK_ROPE_DIM, 2, dtype=jnp.float32) / QK_ROPE_DIM)
    ang = jnp.arange(seq_len, dtype=jnp.float32)[:, None] * inv[None, :]
    return jnp.cos(ang), jnp.sin(ang)


def apply_rope(x, cos, sin):
    cos = cos.astype(x.dtype)
    sin = sin.astype(x.dtype)
    x1, x2 = jnp.split(x, 2, axis=-1)
    return jnp.concatenate([x1 * cos - x2 * sin, x2 * cos + x1 * sin], axis=-1)


def mla_attention(q_nope, q_rope, k_nope, k_rope, v):
    B, S, H, _ = q_nope.shape
    nb = S // Q_BLOCK
    scale = (QK_NOPE_DIM + QK_ROPE_DIM) ** -0.5
    qn = q_nope.reshape(B, nb, Q_BLOCK, H, QK_NOPE_DIM).swapaxes(0, 1)
    qr = q_rope.reshape(B, nb, Q_BLOCK, H, QK_ROPE_DIM).swapaxes(0, 1)

    def block(args):
        qn_b, qr_b = args
        s = jnp.einsum('bqhd,bkhd->bhqk', qn_b, k_nope, preferred_element_type=jnp.float32)
        s = s + jnp.einsum('bqhr,bkr->bhqk', qr_b, k_rope, preferred_element_type=jnp.float32)
        p = jax.nn.softmax(s * scale, axis=-1).astype(v.dtype)
        return jnp.einsum('bhqk,bkhd->bqhd', p, v)

    o = lax.map(block, (qn, qr))
    return o.swapaxes(0, 1).reshape(B, S, H * V_HEAD_DIM)


def encoder_layer(x, c, cos, sin, ada_w, ada_b, norm_mix_g, w_in, q_norm_g, w_q_up,
                  kv_norm_g, w_kv_up, w_attn_o, conv_dw, conv_dw_b, conv_ln_g, conv_ln_b,
                  w_conv_out, w_out, norm_mlp_g, w_mlp_up, w_mlp_down):
    B, S, _ = x.shape
    mod = jax.nn.silu(c) @ ada_w + ada_b
    shift1, scale1, gate1, shift2, scale2, gate2 = jnp.split(mod[:, None, :], N_MOD, axis=-1)

    h = rms_norm(x, norm_mix_g) * (1 + scale1) + shift1
    proj = h @ w_in
    conv_in, q_a, kv_a, k_rope_raw, g_conv, g_attn = jnp.split(proj, COL_SPLITS, axis=-1)

    a, b = jnp.split(conv_in, 2, axis=-1)
    u = a * jax.nn.sigmoid(b)
    z = lax.conv_general_dilated(
        u, conv_dw.reshape(CONV_KERNEL, 1, CONV_WIDTH),
        window_strides=(1,), padding=[(CONV_KERNEL // 2, CONV_KERNEL // 2)],
        dimension_numbers=('NWC', 'WIO', 'NWC'), feature_group_count=CONV_WIDTH) + conv_dw_b
    z = jax.nn.silu(layer_norm(z, conv_ln_g, conv_ln_b))
    y_conv = z @ w_conv_out

    q = (rms_norm(q_a, q_norm_g) @ w_q_up).reshape(B, S, N_HEADS, QK_NOPE_DIM + QK_ROPE_DIM)
    q_nope, q_rope = jnp.split(q, [QK_NOPE_DIM], axis=-1)
    q_rope = apply_rope(q_rope, cos[:, None, :], sin[:, None, :])
    kv = (rms_norm(kv_a, kv_norm_g) @ w_kv_up).reshape(B, S, N_HEADS, QK_NOPE_DIM + V_HEAD_DIM)
    k_nope, v = jnp.split(kv, [QK_NOPE_DIM], axis=-1)
    k_rope = apply_rope(k_rope_raw, cos, sin)
    y_attn = mla_attention(q_nope, q_rope, k_nope, k_rope, v) @ w_attn_o

    mix = (jax.nn.sigmoid(g_conv) * y_conv + jax.nn.sigmoid(g_attn) * y_attn) @ w_out
    x = x + gate1 * mix

    h2 = rms_norm(x, norm_mlp_g) * (1 + scale2) + shift2
    x = x + gate2 * (jnp.square(jax.nn.relu(h2 @ w_mlp_up)) @ w_mlp_down)
    return x


def trunk(x, c, weights, final_g):
    cos, sin = rope_tables(x.shape[1])
    for l in range(DEPTH):
        x = encoder_layer(x, c, cos, sin, *[w[l] for w in weights])
    return rms_norm(x, final_g)


def setup_inputs(seed: int = 0) -> dict:
    key = jax.random.key(seed)
    ks = jax.random.split(key, 24)
    f32 = jnp.float32

    def w(k, shape, fan_in, s=1.0):
        return jax.random.normal(k, shape, f32) * (s * fan_in ** -0.5)

    def gain(k, shape):
        return 1.0 + 0.01 * jax.random.normal(k, shape, f32)

    def bias(k, shape):
        return 0.01 * jax.random.normal(k, shape, f32)

    L, D = DEPTH, D_MODEL
    return {
        "x_prompt": jax.random.normal(ks[0], (BATCH, SEQ, D), f32),
        "x_sample": jax.random.normal(ks[1], (DEC_BATCH, DEC_SEQ, D), f32),
        "c_prompt": jax.random.normal(ks[2], (BATCH, D), f32),
        "c_sample": jax.random.normal(ks[3], (DEC_BATCH, D), f32),
        "ada_w": w(ks[4], (L, D, N_MOD * D), D, 0.2),
        "ada_b": bias(ks[5], (L, N_MOD * D)),
        "norm_mix_g": gain(ks[6], (L, D)),
        "w_in": w(ks[7], (L, D, IN_COLS), D),
        "q_norm_g": gain(ks[8], (L, Q_LORA_RANK)),
        "w_q_up": w(ks[9], (L, Q_LORA_RANK, N_HEADS * (QK_NOPE_DIM + QK_ROPE_DIM)), Q_LORA_RANK),
        "kv_norm_g": gain(ks[10], (L, KV_LORA_RANK)),
        "w_kv_up": w(ks[11], (L, KV_LORA_RANK, N_HEADS * (QK_NOPE_DIM + V_HEAD_DIM)), KV_LORA_RANK),
        "w_attn_o": w(ks[12], (L, ATTN_WIDTH, D), ATTN_WIDTH),
        "conv_dw": w(ks[13], (L, CONV_KERNEL, CONV_WIDTH), CONV_KERNEL),
        "conv_dw_b": bias(ks[14], (L, CONV_WIDTH)),
        "conv_ln_g": gain(ks[15], (L, CONV_WIDTH)),
        "conv_ln_b": bias(ks[16], (L, CONV_WIDTH)),
        "w_conv_out": w(ks[17], (L, CONV_WIDTH, D), CONV_WIDTH),
        "w_out": w(ks[18], (L, D, D), D),
        "norm_mlp_g": gain(ks[19], (L, D)),
        "w_mlp_up": w(ks[20], (L, D, D_FF), D),
        "w_mlp_down": w(ks[21], (L, D_FF, D), D_FF),
        "final_g": gain(ks[22], (D,)),
    }


def reference(x_prompt, x_sample, c_prompt, c_sample, ada_w, ada_b, norm_mix_g, w_in,
              q_norm_g, w_q_up, kv_norm_g, w_kv_up, w_attn_o, conv_dw, conv_dw_b,
              conv_ln_g, conv_ln_b, w_conv_out, w_out, norm_mlp_g, w_mlp_up, w_mlp_down,
              final_g):
    weights = (ada_w, ada_b, norm_mix_g, w_in, q_norm_g, w_q_up, kv_norm_g, w_kv_up,
               w_attn_o, conv_dw, conv_dw_b, conv_ln_g, conv_ln_b, w_conv_out, w_out,
               norm_mlp_g, w_mlp_up, w_mlp_down)
    y_prompt = trunk(x_prompt, c_prompt, weights, final_g)
    y_sample = trunk(x_sample, c_sample, weights, final_g)
    return (y_prompt, y_sample)
```

```python
import functools

import jax
import jax.numpy as jnp
from jax import lax
from jax.experimental import pallas as pl
from jax.experimental.pallas import tpu as pltpu

D_MODEL = 1024
N_HEADS = 8
QK_NOPE_DIM = 64
QK_ROPE_DIM = 64
V_HEAD_DIM = 64
HEAD_DIM = QK_NOPE_DIM + QK_ROPE_DIM
Q_LORA_RANK = 384
KV_LORA_RANK = 256
CONV_WIDTH = 512
CONV_KERNEL = 31
CONV_HALO = 16
D_FF = 4 * D_MODEL
FF_CHUNK = 1024
NORM_EPS = 1e-6
N_MOD = 6
ROPE_BASE = 10000.0
LANES = 128
VMEM_LIMIT_BYTES = 56 * 1024 * 1024

COL_CONV = 0
COL_QA = COL_CONV + 2 * CONV_WIDTH
COL_KVA = COL_QA + Q_LORA_RANK
COL_KR = COL_KVA + KV_LORA_RANK
COL_GATE = COL_KR + LANES
COL_END = COL_GATE + 2 * D_MODEL

F32 = jnp.float32
BF16 = jnp.bfloat16


def _rms(x, g):
    return x * lax.rsqrt(jnp.mean(x * x, axis=-1, keepdims=True) + NORM_EPS) * g


def _dot(a, b):
    return jnp.dot(a, b, preferred_element_type=F32)


def _const_spec(shape):
    return pl.BlockSpec(shape, lambda *_: (0,) * len(shape), pipeline_mode=pl.Buffered(1))


def _rope(t, ct, sa, sb):
    return t * ct + pltpu.roll(t, LANES - QK_ROPE_DIM // 2, 1) * sa + pltpu.roll(t, QK_ROPE_DIM // 2, 1) * sb


def _ada_kernel(c_ref, w_ref, b_ref, o_ref):
    c = c_ref[...]
    sc = (c * jax.nn.sigmoid(c)).astype(BF16)
    o_ref[0] = _dot(sc, w_ref[0].astype(BF16)) + b_ref[0]


def _ada(c_all, ada_w, ada_b):
    n_layers = ada_w.shape[0]
    bp = c_all.shape[0]
    return pl.pallas_call(
        _ada_kernel,
        grid=(n_layers, N_MOD),
        in_specs=[
            pl.BlockSpec((bp, D_MODEL), lambda l, j: (0, 0)),
            pl.BlockSpec((1, D_MODEL, D_MODEL), lambda l, j: (l, 0, j)),
            pl.BlockSpec((1, 1, D_MODEL), lambda l, j: (l, 0, j)),
        ],
        out_specs=pl.BlockSpec((1, bp, D_MODEL), lambda l, j: (l, 0, j)),
        out_shape=jax.ShapeDtypeStruct((n_layers, bp, N_MOD * D_MODEL), F32),
        compiler_params=pltpu.CompilerParams(dimension_semantics=("arbitrary", "arbitrary")),
        name="ada_mod",
    )(c_all, ada_w, ada_b.reshape(n_layers, 1, N_MOD * D_MODEL))


def _inproj_kernel(x_ref, mod_ref, g1_ref, w1_ref, qg_ref, wq_ref, kvg_ref, wk_ref, wv_ref,
                   ct_ref, sa_ref, sb_ref, u_ref, q_ref, k_ref, v_ref, gc_ref, ga_ref):
    x = x_ref[0]
    mod = mod_ref[0]
    h = (_rms(x, g1_ref[...]) * (1.0 + mod[1:2]) + mod[0:1]).astype(BF16)
    ct, sa, sb = ct_ref[...], sa_ref[...], sb_ref[...]

    conv_in = _dot(h, w1_ref[:, COL_CONV:COL_QA])
    u_ref[0] = (conv_in[:, :CONV_WIDTH] * jax.nn.sigmoid(conv_in[:, CONV_WIDTH:])).astype(BF16)

    q_a = _dot(h, w1_ref[:, COL_QA:COL_KVA])
    q = _dot(_rms(q_a, qg_ref[...]).astype(BF16), wq_ref[...])
    scale = HEAD_DIM ** -0.5
    for hd in range(N_HEADS):
        qh = _rope(q[:, hd * HEAD_DIM:(hd + 1) * HEAD_DIM], ct, sa, sb)
        q_ref[0, hd] = (qh * scale).astype(BF16)

    kv_a = _dot(h, w1_ref[:, COL_KVA:COL_KR])
    kv_n = _rms(kv_a, kvg_ref[...]).astype(BF16)
    k_rope = _rope(_dot(h, w1_ref[:, COL_KR:COL_GATE]), ct, sa, sb)
    kv = _dot(kv_n, wk_ref[...])
    lane = lax.broadcasted_iota(jnp.int32, k_rope.shape, 1)
    for hd in range(N_HEADS):
        k_ref[0, hd] = jnp.where(lane < QK_NOPE_DIM, kv[:, hd * HEAD_DIM:(hd + 1) * HEAD_DIM], k_rope).astype(BF16)
    v_ref[0] = _dot(kv_n, wv_ref[...]).astype(BF16)

    gates = _dot(h, w1_ref[:, COL_GATE:COL_END])
    gc_ref[0] = jax.nn.sigmoid(gates[:, :D_MODEL])
    ga_ref[0] = jax.nn.sigmoid(gates[:, D_MODEL:])


def _inproj(x, mod, g1, w1, qg, wq, kvg, wk, wv, ct, sa, sb, *, tm):
    b, s, _ = x.shape
    tok = lambda width: pl.BlockSpec((1, tm, width), lambda bi, i: (bi, i, 0))
    head = pl.BlockSpec((1, N_HEADS, tm, HEAD_DIM), lambda bi, i: (bi, 0, i, 0))
    tab = pl.BlockSpec((tm, LANES), lambda bi, i: (i, 0))
    return pl.pallas_call(
        _inproj_kernel,
        grid=(b, s // tm),
        in_specs=[
            tok(D_MODEL),
            pl.BlockSpec((1, N_MOD, D_MODEL), lambda bi, i: (bi, 0, 0)),
            _const_spec((1, D_MODEL)),
            _const_spec((D_MODEL, COL_END)),
            _const_spec((1, Q_LORA_RANK)),
            _const_spec((Q_LORA_RANK, N_HEADS * HEAD_DIM)),
            _const_spec((1, KV_LORA_RANK)),
            _const_spec((KV_LORA_RANK, N_HEADS * HEAD_DIM)),
            _const_spec((KV_LORA_RANK, N_HEADS * V_HEAD_DIM)),
            tab, tab, tab,
        ],
        out_specs=[tok(CONV_WIDTH), head, head, tok(N_HEADS * V_HEAD_DIM), tok(D_MODEL), tok(D_MODEL)],
        out_shape=[
            jax.ShapeDtypeStruct((b, s, CONV_WIDTH), BF16),
            jax.ShapeDtypeStruct((b, N_HEADS, s, HEAD_DIM), BF16),
            jax.ShapeDtypeStruct((b, N_HEADS, s, HEAD_DIM), BF16),
            jax.ShapeDtypeStruct((b, s, N_HEADS * V_HEAD_DIM), BF16),
            jax.ShapeDtypeStruct((b, s, D_MODEL), F32),
            jax.ShapeDtypeStruct((b, s, D_MODEL), F32),
        ],
        compiler_params=pltpu.CompilerParams(
            dimension_semantics=("arbitrary", "arbitrary"), vmem_limit_bytes=VMEM_LIMIT_BYTES),
        name="in_proj",
    )(x, mod, g1, w1, qg, wq, kvg, wk, wv, ct, sa, sb)


def _conv_kernel(u_ref, prev_ref, next_ref, w_ref, b_ref, lg_ref, lb_ref, z_ref, ext_ref, *, rows):
    i = pl.program_id(1)
    tc = u_ref.shape[1]
    first = i == 0
    last = i == pl.num_programs(1) - 1
    ext_ref[0:CONV_HALO, :] = jnp.where(first, 0.0, prev_ref[0].astype(F32))
    ext_ref[CONV_HALO:CONV_HALO + tc, :] = u_ref[0].astype(F32)
    ext_ref[CONV_HALO + tc:, :] = jnp.where(last, 0.0, next_ref[0].astype(F32))
    w = w_ref[...]
    off = CONV_HALO - CONV_KERNEL // 2
    for r0 in range(0, tc, rows):
        acc = jnp.broadcast_to(b_ref[...], (rows, CONV_WIDTH))
        for k in range(CONV_KERNEL):
            acc = acc + w[k:k + 1] * ext_ref[r0 + k + off:r0 + k + off + rows, :]
        mu = jnp.mean(acc, axis=-1, keepdims=True)
        xc = acc - mu
        y = xc * lax.rsqrt(jnp.mean(xc * xc, axis=-1, keepdims=True) + NORM_EPS) * lg_ref[...] + lb_ref[...]
        z_ref[0, r0:r0 + rows, :] = (y * jax.nn.sigmoid(y)).astype(BF16)


def _conv(u, w, bias, ln_g, ln_b, *, tc, rows=32):
    b, s, _ = u.shape
    nh = tc // CONV_HALO
    last_halo = s // CONV_HALO - 1
    return pl.pallas_call(
        functools.partial(_conv_kernel, rows=rows),
        grid=(b, s // tc),
        in_specs=[
            pl.BlockSpec((1, tc, CONV_WIDTH), lambda bi, i: (bi, i, 0)),
            pl.BlockSpec((1, CONV_HALO, CONV_WIDTH), lambda bi, i: (bi, jnp.maximum(i * nh - 1, 0), 0)),
            pl.BlockSpec((1, CONV_HALO, CONV_WIDTH), lambda bi, i: (bi, jnp.minimum((i + 1) * nh, last_halo), 0)),
            _const_spec((CONV_KERNEL, CONV_WIDTH)),
            _const_spec((1, CONV_WIDTH)),
            _const_spec((1, CONV_WIDTH)),
            _const_spec((1, CONV_WIDTH)),
        ],
        out_specs=pl.BlockSpec((1, tc, CONV_WIDTH), lambda bi, i: (bi, i, 0)),
        out_shape=jax.ShapeDtypeStruct((b, s, CONV_WIDTH), BF16),
        scratch_shapes=[pltpu.VMEM((tc + 2 * CONV_HALO, CONV_WIDTH), F32)],
        compiler_params=pltpu.CompilerParams(dimension_semantics=("arbitrary", "arbitrary")),
        name="conv_branch",
    )(u, u, u, w, bias, ln_g, ln_b)


def _attn_kernel(q_ref, k_ref, v_ref, o_ref, *, tk):
    tq = q_ref.shape[2]
    n_chunks = k_ref.shape[2] // tk
    outs = []
    for hh in range(2):
        q = q_ref[0, hh]

        def body(c, carry, hh=hh, q=q):
            m, l, acc = carry
            start = pl.multiple_of(c * tk, tk)
            ks = k_ref[0, hh, pl.ds(start, tk), :]
            vs = v_ref[0, pl.ds(start, tk), :]
            s = lax.dot_general(q, ks, (((1,), (1,)), ((), ())), preferred_element_type=F32)
            m_new = jnp.maximum(m, jnp.max(s, axis=-1, keepdims=True))
            alpha = jnp.exp(m - m_new)
            p = jnp.exp(s - m_new)
            l = alpha * l + jnp.sum(p, axis=-1, keepdims=True)
            acc = alpha * acc + _dot(p.astype(BF16), vs)
            return m_new, l, acc

        init = (jnp.full((tq, 1), -jnp.inf, F32), jnp.zeros((tq, 1), F32), jnp.zeros((tq, LANES), F32))
        _, l, acc = lax.fori_loop(0, n_chunks, body, init)
        outs.append(acc / l)
    lane = lax.broadcasted_iota(jnp.int32, (tq, LANES), 1)
    o_ref[0] = jnp.where(lane < V_HEAD_DIM, outs[0], outs[1]).astype(BF16)


def _attention(q, k, v, *, tq, tk):
    b, _, s, _ = q.shape
    return pl.pallas_call(
        functools.partial(_attn_kernel, tk=tk),
        grid=(b, N_HEADS // 2, s // tq),
        in_specs=[
            pl.BlockSpec((1, 2, tq, HEAD_DIM), lambda bi, hp, i: (bi, hp, i, 0)),
            pl.BlockSpec((1, 2, s, HEAD_DIM), lambda bi, hp, i: (bi, hp, 0, 0)),
            pl.BlockSpec((1, s, LANES), lambda bi, hp, i: (bi, 0, hp)),
        ],
        out_specs=pl.BlockSpec((1, tq, LANES), lambda bi, hp, i: (bi, i, hp)),
        out_shape=jax.ShapeDtypeStruct((b, s, N_HEADS * V_HEAD_DIM), BF16),
        compiler_params=pltpu.CompilerParams(
            dimension_semantics=("arbitrary", "arbitrary", "arbitrary"), vmem_limit_bytes=VMEM_LIMIT_BYTES),
        name="attention",
    )(q, k, v)


def _outproj_kernel(x_ref, mod_ref, o_ref, z_ref, gc_ref, ga_ref, wao_ref, wco_ref, wout_ref, g2_ref,
                    wup_ref, wdn_ref, fg_ref, y_ref, *, final):
    x = x_ref[0]
    mod = mod_ref[0]
    y_attn = _dot(o_ref[0], wao_ref[...])
    y_conv = _dot(z_ref[0], wco_ref[...])
    mix_in = (gc_ref[0] * y_conv + ga_ref[0] * y_attn).astype(BF16)
    x1 = x + mod[2:3] * _dot(mix_in, wout_ref[...])
    h2 = (_rms(x1, g2_ref[...]) * (1.0 + mod[4:5]) + mod[3:4]).astype(BF16)
    acc = jnp.zeros(x.shape, F32)
    for c in range(0, D_FF, FF_CHUNK):
        up = _dot(h2, wup_ref[:, c:c + FF_CHUNK])
        acc = acc + _dot(jnp.square(jnp.maximum(up, 0.0)).astype(BF16), wdn_ref[c:c + FF_CHUNK, :])
    x2 = x1 + mod[5:6] * acc
    if final:
        x2 = _rms(x2, fg_ref[...])
    y_ref[0] = x2


def _outproj(x, mod, o, z, gc, ga, wao, wco, wout, g2, wup, wdn, fg, *, tm, final):
    b, s, _ = x.shape
    tok = lambda width: pl.BlockSpec((1, tm, width), lambda bi, i: (bi, i, 0))
    return pl.pallas_call(
        functools.partial(_outproj_kernel, final=final),
        grid=(b, s // tm),
        in_specs=[
            tok(D_MODEL),
            pl.BlockSpec((1, N_MOD, D_MODEL), lambda bi, i: (bi, 0, 0)),
            tok(N_HEADS * V_HEAD_DIM), tok(CONV_WIDTH), tok(D_MODEL), tok(D_MODEL),
            _const_spec((N_HEADS * V_HEAD_DIM, D_MODEL)),
            _const_spec((CONV_WIDTH, D_MODEL)),
            _const_spec((D_MODEL, D_MODEL)),
            _const_spec((1, D_MODEL)),
            _const_spec((D_MODEL, D_FF)),
            _const_spec((D_FF, D_MODEL)),
            _const_spec((1, D_MODEL)),
        ],
        out_specs=tok(D_MODEL),
        out_shape=jax.ShapeDtypeStruct((b, s, D_MODEL), F32),
        compiler_params=pltpu.CompilerParams(
            dimension_semantics=("arbitrary", "arbitrary"), vmem_limit_bytes=VMEM_LIMIT_BYTES),
        name="out_proj_mlp",
    )(x, mod, o, z, gc, ga, wao, wco, wout, g2, wup, wdn, fg)


def _rope_tables(seq_len):
    inv = ROPE_BASE ** (-jnp.arange(0, QK_ROPE_DIM, 2, dtype=F32) / QK_ROPE_DIM)
    ang = jnp.arange(seq_len, dtype=F32)[:, None] * inv[None, :]
    cos, sin = jnp.cos(ang), jnp.sin(ang)
    one = jnp.ones((seq_len, QK_NOPE_DIM), F32)
    zero = jnp.zeros((seq_len, QK_NOPE_DIM), F32)
    half = jnp.zeros_like(sin)
    ct = jnp.concatenate([one, cos, cos], axis=-1)
    sa = jnp.concatenate([zero, -sin, half], axis=-1)
    sb = jnp.concatenate([zero, half, sin], axis=-1)
    return ct, sa, sb


def _pack_w_in(w_in):
    c0 = 2 * CONV_WIDTH
    c1 = c0 + Q_LORA_RANK
    c2 = c1 + KV_LORA_RANK
    c3 = c2 + QK_ROPE_DIM
    pad = jnp.zeros((w_in.shape[0], LANES - QK_ROPE_DIM), w_in.dtype)
    return jnp.concatenate([w_in[:, :c2], pad, w_in[:, c2:c3], w_in[:, c3:]], axis=-1).astype(BF16)


def _v_columns(w_kv_up):
    w = w_kv_up.reshape(KV_LORA_RANK, N_HEADS, QK_NOPE_DIM + V_HEAD_DIM)[:, :, QK_NOPE_DIM:]
    return w.reshape(KV_LORA_RANK, N_HEADS * V_HEAD_DIM).astype(BF16)


def _trunk(x, mods, layer_weights, final_g, *, tm, tc, tq, tk):
    ct, sa, sb = _rope_tables(x.shape[1])
    n_layers = len(layer_weights)
    for l, w in enumerate(layer_weights):
        mod = mods[l]
        u, q, k, v, gc, ga = _inproj(x, mod, w["g1"], w["w1"], w["qg"], w["wq"], w["kvg"], w["wk"], w["wv"],
                                     ct, sa, sb, tm=tm)
        z = _conv(u, w["conv_w"], w["conv_b"], w["ln_g"], w["ln_b"], tc=tc)
        o = _attention(q, k, v, tq=tq, tk=tk)
        x = _outproj(x, mod, o, z, gc, ga, w["wao"], w["wco"], w["wout"], w["g2"], w["wup"], w["wdn"],
                     final_g, tm=tm, final=(l == n_layers - 1))
    return x


def kernel(x_prompt, x_sample, c_prompt, c_sample, ada_w, ada_b, norm_mix_g, w_in, q_norm_g, w_q_up, kv_norm_g, w_kv_up, w_attn_o, conv_dw, conv_dw_b, conv_ln_g, conv_ln_b, w_conv_out, w_out, norm_mlp_g, w_mlp_up, w_mlp_down, final_g):
    n_layers = ada_w.shape[0]
    bp, bs = c_prompt.shape[0], c_sample.shape[0]
    pad_rows = -(bp + bs) % 8
    c_all = jnp.concatenate([c_prompt, c_sample, jnp.zeros((pad_rows, D_MODEL), F32)], axis=0)
    mods = _ada(c_all, ada_w, ada_b)
    mods_p = mods[:, :bp].reshape(n_layers, bp, N_MOD, D_MODEL)
    mods_s = mods[:, bp:bp + bs].reshape(n_layers, bs, N_MOD, D_MODEL)

    row = lambda a: a.reshape(1, -1)
    layer_weights = []
    for l in range(n_layers):
        layer_weights.append(dict(
            g1=row(norm_mix_g[l]), w1=_pack_w_in(w_in[l]), qg=row(q_norm_g[l]), wq=w_q_up[l].astype(BF16),
            kvg=row(kv_norm_g[l]), wk=w_kv_up[l].astype(BF16), wv=_v_columns(w_kv_up[l]),
            conv_w=conv_dw[l], conv_b=row(conv_dw_b[l]), ln_g=row(conv_ln_g[l]), ln_b=row(conv_ln_b[l]),
            wao=w_attn_o[l].astype(BF16), wco=w_conv_out[l].astype(BF16), wout=w_out[l].astype(BF16),
            g2=row(norm_mlp_g[l]), wup=w_mlp_up[l].astype(BF16), wdn=w_mlp_down[l].astype(BF16)))
    fg = row(final_g)
    tiles = dict(tm=256, tc=256, tq=512, tk=512)
    y_prompt = _trunk(x_prompt, mods_p, layer_weights, fg, **tiles)
    y_sample = _trunk(x_sample, mods_s, layer_weights, fg, **tiles)
    return (y_prompt, y_sample)
```

```python
import functools

import jax
import jax.numpy as jnp
from jax import lax
from jax.experimental import pallas as pl
from jax.experimental.pallas import tpu as pltpu

D_MODEL = 1024
N_HEADS = 8
QK_NOPE_DIM = 64
QK_ROPE_DIM = 64
V_HEAD_DIM = 64
HEAD_DIM = QK_NOPE_DIM + QK_ROPE_DIM
Q_LORA_RANK = 384
KV_LORA_RANK = 256
CONV_WIDTH = 512
CONV_KERNEL = 31
CONV_HALO = 16
D_FF = 4 * D_MODEL
FF_CHUNK = 1024
NORM_EPS = 1e-6
QK_SCALE_LOG2E = (QK_NOPE_DIM + QK_ROPE_DIM) ** -0.5 * 1.4426950408889634
N_MOD = 6
ROPE_BASE = 10000.0
LANES = 128
VMEM_LIMIT_BYTES = 56 * 1024 * 1024

COL_CONV = 0
COL_QA = COL_CONV + 2 * CONV_WIDTH
COL_KVA = COL_QA + Q_LORA_RANK
COL_KR = COL_KVA + KV_LORA_RANK
COL_GATE = COL_KR + LANES
COL_END = COL_GATE + 2 * D_MODEL

F32 = jnp.float32
BF16 = jnp.bfloat16


def _rms(x, g):
    return x * lax.rsqrt(jnp.mean(x * x, axis=-1, keepdims=True) + NORM_EPS) * g


def _dot(a, b):
    return jnp.dot(a, b, preferred_element_type=F32)


def _const_spec(shape):
    return pl.BlockSpec(shape, lambda *_: (0,) * len(shape), pipeline_mode=pl.Buffered(1))


def _rope(t, ct, sa, sb):
    return t * ct + pltpu.roll(t, LANES - QK_ROPE_DIM // 2, 1) * sa + pltpu.roll(t, QK_ROPE_DIM // 2, 1) * sb


def _ada_kernel(c_ref, w_ref, b_ref, o_ref):
    c = c_ref[...]
    sc = (c * jax.nn.sigmoid(c)).astype(BF16)
    o_ref[0] = _dot(sc, w_ref[0].astype(BF16)) + b_ref[0]


def _ada(c_all, ada_w, ada_b):
    n_layers = ada_w.shape[0]
    bp = c_all.shape[0]
    return pl.pallas_call(
        _ada_kernel,
        grid=(n_layers, N_MOD),
        in_specs=[
            pl.BlockSpec((bp, D_MODEL), lambda l, j: (0, 0)),
            pl.BlockSpec((1, D_MODEL, D_MODEL), lambda l, j: (l, 0, j)),
            pl.BlockSpec((1, 1, D_MODEL), lambda l, j: (l, 0, j)),
        ],
        out_specs=pl.BlockSpec((1, bp, D_MODEL), lambda l, j: (l, 0, j)),
        out_shape=jax.ShapeDtypeStruct((n_layers, bp, N_MOD * D_MODEL), F32),
        compiler_params=pltpu.CompilerParams(dimension_semantics=("arbitrary", "arbitrary")),
        name="ada_mod",
    )(c_all, ada_w, ada_b.reshape(n_layers, 1, N_MOD * D_MODEL))


def _inproj_kernel(x_ref, mod_ref, g1_ref, w1_ref, qg_ref, wq_ref, kvg_ref, wk_ref,
                   ct_ref, sa_ref, sb_ref, u_ref, q_ref, k_ref, vt_ref, gc_ref, ga_ref):
    x = x_ref[0]
    tm = x.shape[0]
    mod = mod_ref[0]
    h = (_rms(x, g1_ref[...]) * (1.0 + mod[1:2]) + mod[0:1]).astype(BF16)
    ct, sa, sb = ct_ref[...], sa_ref[...], sb_ref[...]

    conv_in = _dot(h, w1_ref[:, COL_CONV:COL_QA])
    u_ref[0] = (conv_in[:, :CONV_WIDTH] * jax.nn.sigmoid(conv_in[:, CONV_WIDTH:])).astype(BF16)

    q_a = _dot(h, w1_ref[:, COL_QA:COL_KVA])
    q = _dot(_rms(q_a, qg_ref[...]).astype(BF16), wq_ref[...])
    for hd in range(N_HEADS):
        qh = _rope(q[:, hd * HEAD_DIM:(hd + 1) * HEAD_DIM], ct, sa, sb)
        q_ref[0, hd] = (qh * QK_SCALE_LOG2E).astype(BF16)

    kv_a = _dot(h, w1_ref[:, COL_KVA:COL_KR])
    kv_n = _rms(kv_a, kvg_ref[...]).astype(BF16)
    k_rope = _rope(_dot(h, w1_ref[:, COL_KR:COL_GATE]), ct, sa, sb)
    kv = _dot(kv_n, wk_ref[...])
    lane = lax.broadcasted_iota(jnp.int32, k_rope.shape, 1)
    ones = jnp.ones((HEAD_DIM - V_HEAD_DIM, tm), F32)
    for hd in range(N_HEADS):
        kv_h = kv[:, hd * HEAD_DIM:(hd + 1) * HEAD_DIM]
        k_ref[0, hd] = jnp.where(lane < QK_NOPE_DIM, kv_h, k_rope).astype(BF16)
        vt_ref[0, hd, 0] = jnp.concatenate([kv_h.T[QK_NOPE_DIM:], ones], axis=0).astype(BF16)

    gates = _dot(h, w1_ref[:, COL_GATE:COL_END])
    gc_ref[0] = jax.nn.sigmoid(gates[:, :D_MODEL])
    ga_ref[0] = jax.nn.sigmoid(gates[:, D_MODEL:])


def _inproj(x, mod, g1, w1, qg, wq, kvg, wk, ct, sa, sb, *, tm, tk):
    b, s, _ = x.shape
    per_chunk = tk // tm
    tok = lambda width: pl.BlockSpec((1, tm, width), lambda bi, i: (bi, i, 0))
    head = pl.BlockSpec((1, N_HEADS, tm, HEAD_DIM), lambda bi, i: (bi, 0, i, 0))
    head_t = pl.BlockSpec((1, N_HEADS, 1, HEAD_DIM, tm), lambda bi, i: (bi, 0, i // per_chunk, 0, i % per_chunk))
    tab = pl.BlockSpec((tm, LANES), lambda bi, i: (i, 0))
    return pl.pallas_call(
        _inproj_kernel,
        grid=(b, s // tm),
        in_specs=[
            tok(D_MODEL),
            pl.BlockSpec((1, N_MOD, D_MODEL), lambda bi, i: (bi, 0, 0)),
            _const_spec((1, D_MODEL)),
            _const_spec((D_MODEL, COL_END)),
            _const_spec((1, Q_LORA_RANK)),
            _const_spec((Q_LORA_RANK, N_HEADS * HEAD_DIM)),
            _const_spec((1, KV_LORA_RANK)),
            _const_spec((KV_LORA_RANK, N_HEADS * HEAD_DIM)),
            tab, tab, tab,
        ],
        out_specs=[tok(CONV_WIDTH), head, head, head_t, tok(D_MODEL), tok(D_MODEL)],
        out_shape=[
            jax.ShapeDtypeStruct((b, s, CONV_WIDTH), BF16),
            jax.ShapeDtypeStruct((b, N_HEADS, s, HEAD_DIM), BF16),
            jax.ShapeDtypeStruct((b, N_HEADS, s, HEAD_DIM), BF16),
            jax.ShapeDtypeStruct((b, N_HEADS, s // tk, HEAD_DIM, tk), BF16),
            jax.ShapeDtypeStruct((b, s, D_MODEL), F32),
            jax.ShapeDtypeStruct((b, s, D_MODEL), F32),
        ],
        compiler_params=pltpu.CompilerParams(
            dimension_semantics=("arbitrary", "arbitrary"), vmem_limit_bytes=VMEM_LIMIT_BYTES),
        name="in_proj",
    )(x, mod, g1, w1, qg, wq, kvg, wk, ct, sa, sb)


def _conv_kernel(u_ref, prev_ref, next_ref, w_ref, b_ref, lg_ref, lb_ref, z_ref, ext_ref, *, rows):
    i = pl.program_id(1)
    tc = u_ref.shape[1]
    first = i == 0
    last = i == pl.num_programs(1) - 1
    ext_ref[0:CONV_HALO, :] = jnp.where(first, 0.0, prev_ref[0].astype(F32))
    ext_ref[CONV_HALO:CONV_HALO + tc, :] = u_ref[0].astype(F32)
    ext_ref[CONV_HALO + tc:, :] = jnp.where(last, 0.0, next_ref[0].astype(F32))
    w = w_ref[...]
    off = CONV_HALO - CONV_KERNEL // 2
    for r0 in range(0, tc, rows):
        acc = jnp.broadcast_to(b_ref[...], (rows, CONV_WIDTH))
        for k in range(CONV_KERNEL):
            acc = acc + w[k:k + 1] * ext_ref[r0 + k + off:r0 + k + off + rows, :]
        mu = jnp.mean(acc, axis=-1, keepdims=True)
        xc = acc - mu
        y = xc * lax.rsqrt(jnp.mean(xc * xc, axis=-1, keepdims=True) + NORM_EPS) * lg_ref[...] + lb_ref[...]
        z_ref[0, r0:r0 + rows, :] = (y * jax.nn.sigmoid(y)).astype(BF16)


def _conv(u, w, bias, ln_g, ln_b, *, tc, rows=32):
    b, s, _ = u.shape
    nh = tc // CONV_HALO
    last_halo = s // CONV_HALO - 1
    return pl.pallas_call(
        functools.partial(_conv_kernel, rows=rows),
        grid=(b, s // tc),
        in_specs=[
            pl.BlockSpec((1, tc, CONV_WIDTH), lambda bi, i: (bi, i, 0)),
            pl.BlockSpec((1, CONV_HALO, CONV_WIDTH), lambda bi, i: (bi, jnp.maximum(i * nh - 1, 0), 0)),
            pl.BlockSpec((1, CONV_HALO, CONV_WIDTH), lambda bi, i: (bi, jnp.minimum((i + 1) * nh, last_halo), 0)),
            _const_spec((CONV_KERNEL, CONV_WIDTH)),
            _const_spec((1, CONV_WIDTH)),
            _const_spec((1, CONV_WIDTH)),
            _const_spec((1, CONV_WIDTH)),
        ],
        out_specs=pl.BlockSpec((1, tc, CONV_WIDTH), lambda bi, i: (bi, i, 0)),
        out_shape=jax.ShapeDtypeStruct((b, s, CONV_WIDTH), BF16),
        scratch_shapes=[pltpu.VMEM((tc + 2 * CONV_HALO, CONV_WIDTH), F32)],
        compiler_params=pltpu.CompilerParams(dimension_semantics=("arbitrary", "arbitrary")),
        name="conv_branch",
    )(u, u, u, w, bias, ln_g, ln_b)


def _attn_kernel(q_ref, k_ref, vt_ref, o_ref, s0_ref, s1_ref):
    tq = q_ref.shape[2]
    n_chunks, _, tk = vt_ref.shape[2:]
    qs = [q_ref[0, hh] for hh in range(2)]

    def scores(s_ref, hh, c):
        kc = k_ref[0, hh, pl.ds(pl.multiple_of(c * tk, tk), tk), :]
        s_ref[hh] = lax.dot_general(kc, qs[hh], (((1,), (1,)), ((), ())), preferred_element_type=F32)

    def update(s_ref, hh, c, state):
        m, acc = state
        st = s_ref[hh]
        m_new = jnp.maximum(m, jnp.max(st, axis=0, keepdims=True))
        alpha = jnp.exp2(m - m_new)
        pt = jnp.exp2(st - m_new).astype(BF16)
        return m_new, alpha * acc + _dot(vt_ref[0, hh, c], pt)

    def step(c, cur_ref, nxt_ref, state, with_next=True):
        new = []
        for hh in range(2):
            if with_next:
                scores(nxt_ref, hh, c + 1)
            new.append(update(cur_ref, hh, c, state[hh]))
        return tuple(new)

    def body(j, state):
        state = step(2 * j, s0_ref, s1_ref, state)
        return step(2 * j + 1, s1_ref, s0_ref, state)

    for hh in range(2):
        scores(s0_ref, hh, 0)
    state = tuple((jnp.full((1, tq), -jnp.inf, F32), jnp.zeros((HEAD_DIM, tq), F32)) for _ in range(2))
    state = lax.fori_loop(0, n_chunks // 2 - 1, body, state)
    state = step(n_chunks - 2, s0_ref, s1_ref, state)
    state = step(n_chunks - 1, s1_ref, s0_ref, state, with_next=False)
    ot = jnp.concatenate([acc[:V_HEAD_DIM] / acc[V_HEAD_DIM:V_HEAD_DIM + 1] for _, acc in state], axis=0)
    o_ref[0] = ot.T.astype(BF16)


def _attention(q, k, vt, *, tq):
    b, _, s, _ = q.shape
    n_chunks, _, tk = vt.shape[2:]
    return pl.pallas_call(
        _attn_kernel,
        grid=(b, N_HEADS // 2, s // tq),
        in_specs=[
            pl.BlockSpec((1, 2, tq, HEAD_DIM), lambda bi, hp, i: (bi, hp, i, 0)),
            pl.BlockSpec((1, 2, s, HEAD_DIM), lambda bi, hp, i: (bi, hp, 0, 0)),
            pl.BlockSpec((1, 2, n_chunks, HEAD_DIM, tk), lambda bi, hp, i: (bi, hp, 0, 0, 0)),
        ],
        out_specs=pl.BlockSpec((1, tq, LANES), lambda bi, hp, i: (bi, i, hp)),
        out_shape=jax.ShapeDtypeStruct((b, s, N_HEADS * V_HEAD_DIM), BF16),
        scratch_shapes=[pltpu.VMEM((2, tk, tq), F32), pltpu.VMEM((2, tk, tq), F32)],
        compiler_params=pltpu.CompilerParams(
            dimension_semantics=("arbitrary", "arbitrary", "arbitrary"), vmem_limit_bytes=VMEM_LIMIT_BYTES),
        name="attention",
    )(q, k, vt)


def _outproj_kernel(x_ref, mod_ref, o_ref, z_ref, gc_ref, ga_ref, wao_ref, wco_ref, wout_ref, g2_ref,
                    wup_ref, wdn_ref, fg_ref, y_ref, *, final):
    x = x_ref[0]
    mod = mod_ref[0]
    y_attn = _dot(o_ref[0], wao_ref[...])
    y_conv = _dot(z_ref[0], wco_ref[...])
    mix_in = (gc_ref[0] * y_conv + ga_ref[0] * y_attn).astype(BF16)
    x1 = x + mod[2:3] * _dot(mix_in, wout_ref[...])
    h2 = (_rms(x1, g2_ref[...]) * (1.0 + mod[4:5]) + mod[3:4]).astype(BF16)
    acc = jnp.zeros(x.shape, F32)
    for c in range(0, D_FF, FF_CHUNK):
        up = _dot(h2, wup_ref[:, c:c + FF_CHUNK])
        acc = acc + _dot(jnp.square(jnp.maximum(up, 0.0)).astype(BF16), wdn_ref[c:c + FF_CHUNK, :])
    x2 = x1 + mod[5:6] * acc
    if final:
        x2 = _rms(x2, fg_ref[...])
    y_ref[0] = x2


def _outproj(x, mod, o, z, gc, ga, wao, wco, wout, g2, wup, wdn, fg, *, tm, final):
    b, s, _ = x.shape
    tok = lambda width: pl.BlockSpec((1, tm, width), lambda bi, i: (bi, i, 0))
    return pl.pallas_call(
        functools.partial(_outproj_kernel, final=final),
        grid=(b, s // tm),
        in_specs=[
            tok(D_MODEL),
            pl.BlockSpec((1, N_MOD, D_MODEL), lambda bi, i: (bi, 0, 0)),
            tok(N_HEADS * V_HEAD_DIM), tok(CONV_WIDTH), tok(D_MODEL), tok(D_MODEL),
            _const_spec((N_HEADS * V_HEAD_DIM, D_MODEL)),
            _const_spec((CONV_WIDTH, D_MODEL)),
            _const_spec((D_MODEL, D_MODEL)),
            _const_spec((1, D_MODEL)),
            _const_spec((D_MODEL, D_FF)),
            _const_spec((D_FF, D_MODEL)),
            _const_spec((1, D_MODEL)),
        ],
        out_specs=tok(D_MODEL),
        out_shape=jax.ShapeDtypeStruct((b, s, D_MODEL), F32),
        compiler_params=pltpu.CompilerParams(
            dimension_semantics=("arbitrary", "arbitrary"), vmem_limit_bytes=VMEM_LIMIT_BYTES),
        name="out_proj_mlp",
    )(x, mod, o, z, gc, ga, wao, wco, wout, g2, wup, wdn, fg)


def _rope_tables(seq_len):
    inv = ROPE_BASE ** (-jnp.arange(0, QK_ROPE_DIM, 2, dtype=F32) / QK_ROPE_DIM)
    ang = jnp.arange(seq_len, dtype=F32)[:, None] * inv[None, :]
    cos, sin = jnp.cos(ang), jnp.sin(ang)
    one = jnp.ones((seq_len, QK_NOPE_DIM), F32)
    zero = jnp.zeros((seq_len, QK_NOPE_DIM), F32)
    half = jnp.zeros_like(sin)
    ct = jnp.concatenate([one, cos, cos], axis=-1)
    sa = jnp.concatenate([zero, -sin, half], axis=-1)
    sb = jnp.concatenate([zero, half, sin], axis=-1)
    return ct, sa, sb


def _pack_w_in(w_in):
    c0 = 2 * CONV_WIDTH
    c1 = c0 + Q_LORA_RANK
    c2 = c1 + KV_LORA_RANK
    c3 = c2 + QK_ROPE_DIM
    pad = jnp.zeros((w_in.shape[0], LANES - QK_ROPE_DIM), w_in.dtype)
    return jnp.concatenate([w_in[:, :c2], pad, w_in[:, c2:c3], w_in[:, c3:]], axis=-1).astype(BF16)


def _trunk(x, mods, layer_weights, final_g, *, tm, tc, tq, tk):
    ct, sa, sb = _rope_tables(x.shape[1])
    n_layers = len(layer_weights)
    for l, w in enumerate(layer_weights):
        mod = mods[l]
        u, q, k, vt, gc, ga = _inproj(x, mod, w["g1"], w["w1"], w["qg"], w["wq"], w["kvg"], w["wk"],
                                      ct, sa, sb, tm=tm, tk=tk)
        z = _conv(u, w["conv_w"], w["conv_b"], w["ln_g"], w["ln_b"], tc=tc)
        o = _attention(q, k, vt, tq=tq)
        x = _outproj(x, mod, o, z, gc, ga, w["wao"], w["wco"], w["wout"], w["g2"], w["wup"], w["wdn"],
                     final_g, tm=tm, final=(l == n_layers - 1))
    return x


def kernel(x_prompt, x_sample, c_prompt, c_sample, ada_w, ada_b, norm_mix_g, w_in, q_norm_g, w_q_up, kv_norm_g, w_kv_up, w_attn_o, conv_dw, conv_dw_b, conv_ln_g, conv_ln_b, w_conv_out, w_out, norm_mlp_g, w_mlp_up, w_mlp_down, final_g):
    n_layers = ada_w.shape[0]
    bp, bs = c_prompt.shape[0], c_sample.shape[0]
    pad_rows = -(bp + bs) % 8
    c_all = jnp.concatenate([c_prompt, c_sample, jnp.zeros((pad_rows, D_MODEL), F32)], axis=0)
    mods = _ada(c_all, ada_w, ada_b)
    mods_p = mods[:, :bp].reshape(n_layers, bp, N_MOD, D_MODEL)
    mods_s = mods[:, bp:bp + bs].reshape(n_layers, bs, N_MOD, D_MODEL)

    row = lambda a: a.reshape(1, -1)
    layer_weights = []
    for l in range(n_layers):
        layer_weights.append(dict(
            g1=row(norm_mix_g[l]), w1=_pack_w_in(w_in[l]), qg=row(q_norm_g[l]), wq=w_q_up[l].astype(BF16),
            kvg=row(kv_norm_g[l]), wk=w_kv_up[l].astype(BF16),
            conv_w=conv_dw[l], conv_b=row(conv_dw_b[l]), ln_g=row(conv_ln_g[l]), ln_b=row(conv_ln_b[l]),
            wao=w_attn_o[l].astype(BF16), wco=w_conv_out[l].astype(BF16), wout=w_out[l].astype(BF16),
            g2=row(norm_mlp_g[l]), wup=w_mlp_up[l].astype(BF16), wdn=w_mlp_down[l].astype(BF16)))
    fg = row(final_g)
    tiles = dict(tm=256, tc=256, tq=512, tk=512)
    y_prompt = _trunk(x_prompt, mods_p, layer_weights, fg, **tiles)
    y_sample = _trunk(x_sample, mods_s, layer_weights, fg, **tiles)
    return (y_prompt, y_sample)
```

```python
import functools

import jax
import jax.numpy as jnp
from jax import lax
from jax.experimental import pallas as pl
from jax.experimental.pallas import tpu as pltpu

D_MODEL = 1024
N_HEADS = 8
QK_NOPE_DIM = 64
QK_ROPE_DIM = 64
V_HEAD_DIM = 64
HEAD_DIM = QK_NOPE_DIM + QK_ROPE_DIM
Q_LORA_RANK = 384
KV_LORA_RANK = 256
CONV_WIDTH = 512
CONV_KERNEL = 31
CONV_HALO = 16
D_FF = 4 * D_MODEL
FF_CHUNK = 1024
NORM_EPS = 1e-6
QK_SCALE_LOG2E = (QK_NOPE_DIM + QK_ROPE_DIM) ** -0.5 * 1.4426950408889634
N_MOD = 6
ROPE_BASE = 10000.0
LANES = 128
SUBLANES = 8
VMEM_LIMIT_BYTES = 56 * 1024 * 1024

COL_CONV = 0
COL_QA = COL_CONV + 2 * CONV_WIDTH
COL_KVA = COL_QA + Q_LORA_RANK
COL_KR = COL_KVA + KV_LORA_RANK
COL_GATE = COL_KR + LANES
COL_END = COL_GATE + 2 * D_MODEL

F32 = jnp.float32
BF16 = jnp.bfloat16


def _rms(x, g):
    return x * lax.rsqrt(jnp.mean(x * x, axis=-1, keepdims=True) + NORM_EPS) * g


def _dot(a, b):
    return jnp.dot(a, b, preferred_element_type=F32)


def _const_spec(shape):
    return pl.BlockSpec(shape, lambda *_: (0,) * len(shape), pipeline_mode=pl.Buffered(1))


def _rope(t, ct, sa, sb):
    return t * ct + pltpu.roll(t, LANES - QK_ROPE_DIM // 2, 1) * sa + pltpu.roll(t, QK_ROPE_DIM // 2, 1) * sb


def _ada_kernel(c_ref, w_ref, b_ref, o_ref):
    c = c_ref[...]
    sc = (c * jax.nn.sigmoid(c)).astype(BF16)
    o_ref[0] = _dot(sc, w_ref[0].astype(BF16)) + b_ref[0]


def _ada(c_all, ada_w, ada_b):
    n_layers = ada_w.shape[0]
    bp = c_all.shape[0]
    return pl.pallas_call(
        _ada_kernel,
        grid=(n_layers, N_MOD),
        in_specs=[
            pl.BlockSpec((bp, D_MODEL), lambda l, j: (0, 0)),
            pl.BlockSpec((1, D_MODEL, D_MODEL), lambda l, j: (l, 0, j)),
            pl.BlockSpec((1, 1, D_MODEL), lambda l, j: (l, 0, j)),
        ],
        out_specs=pl.BlockSpec((1, bp, D_MODEL), lambda l, j: (l, 0, j)),
        out_shape=jax.ShapeDtypeStruct((n_layers, bp, N_MOD * D_MODEL), F32),
        compiler_params=pltpu.CompilerParams(dimension_semantics=("arbitrary", "arbitrary")),
        name="ada_mod",
    )(c_all, ada_w, ada_b.reshape(n_layers, 1, N_MOD * D_MODEL))


def _conv_branch(ext_ref, sh_ref, w_ref, b_ref, lg_ref, lb_ref, z_ref, *, rows):
    tm = z_ref.shape[1]
    n_sh = sh_ref.shape[1]
    for s in range(1, SUBLANES):
        sh_ref[s - 1] = ext_ref[s:s + n_sh, :]
    w = w_ref[...]
    off = CONV_HALO - CONV_KERNEL // 2
    for r0 in range(0, tm, rows):
        acc = jnp.broadcast_to(b_ref[...], (rows, CONV_WIDTH))
        for k in range(CONV_KERNEL):
            base, s = divmod(k + off, SUBLANES)
            lo = r0 + base * SUBLANES
            tap = ext_ref[lo:lo + rows, :] if s == 0 else sh_ref[s - 1, lo:lo + rows, :]
            acc = acc + w[k:k + 1] * tap
        mu = jnp.mean(acc, axis=-1, keepdims=True)
        xc = acc - mu
        y = xc * lax.rsqrt(jnp.mean(xc * xc, axis=-1, keepdims=True) + NORM_EPS) * lg_ref[...] + lb_ref[...]
        z_ref[0, r0:r0 + rows, :] = (y * jax.nn.sigmoid(y)).astype(BF16)


def _inproj_kernel(x_ref, xp_ref, xn_ref, mod_ref, g1_ref, w1_ref, qg_ref, wq_ref, kvg_ref, wk_ref,
                   ct_ref, sa_ref, sb_ref, cw_ref, cb_ref, lg_ref, lb_ref,
                   z_ref, q_ref, k_ref, vt_ref, gc_ref, ga_ref, ext_ref, sh_ref, *, conv_rows):
    i = pl.program_id(1)
    tm = x_ref.shape[1]
    mod = mod_ref[0]
    norm_mod = lambda x: (_rms(x, g1_ref[...]) * (1.0 + mod[1:2]) + mod[0:1]).astype(BF16)
    h_ext = jnp.concatenate([norm_mod(xp_ref[0]), norm_mod(x_ref[0]), norm_mod(xn_ref[0])], axis=0)
    h = h_ext[CONV_HALO:CONV_HALO + tm]
    ct, sa, sb = ct_ref[...], sa_ref[...], sb_ref[...]

    conv_in = _dot(h_ext, w1_ref[:, COL_CONV:COL_QA])
    u = conv_in[:, :CONV_WIDTH] * jax.nn.sigmoid(conv_in[:, CONV_WIDTH:])
    row = lax.broadcasted_iota(jnp.int32, (tm + 2 * CONV_HALO, 1), 0)
    inside = jnp.logical_and(jnp.logical_or(row >= CONV_HALO, i > 0),
                             jnp.logical_or(row < CONV_HALO + tm, i < pl.num_programs(1) - 1))
    ext_ref[...] = jnp.where(inside, u, 0.0)
    _conv_branch(ext_ref, sh_ref, cw_ref, cb_ref, lg_ref, lb_ref, z_ref, rows=conv_rows)

    q_a = _dot(h, w1_ref[:, COL_QA:COL_KVA])
    q = _dot(_rms(q_a, qg_ref[...]).astype(BF16), wq_ref[...])
    for hd in range(N_HEADS):
        qh = _rope(q[:, hd * HEAD_DIM:(hd + 1) * HEAD_DIM], ct, sa, sb)
        q_ref[0, hd] = (qh * QK_SCALE_LOG2E).astype(BF16)

    kv_a = _dot(h, w1_ref[:, COL_KVA:COL_KR])
    kv_n = _rms(kv_a, kvg_ref[...]).astype(BF16)
    k_rope = _rope(_dot(h, w1_ref[:, COL_KR:COL_GATE]), ct, sa, sb)
    kv = _dot(kv_n, wk_ref[...])
    lane = lax.broadcasted_iota(jnp.int32, k_rope.shape, 1)
    ones = jnp.ones((HEAD_DIM - V_HEAD_DIM, tm), F32)
    for hd in range(N_HEADS):
        kv_h = kv[:, hd * HEAD_DIM:(hd + 1) * HEAD_DIM]
        k_ref[0, hd] = jnp.where(lane < QK_NOPE_DIM, kv_h, k_rope).astype(BF16)
        vt_ref[0, hd, 0] = jnp.concatenate([kv_h.T[QK_NOPE_DIM:], ones], axis=0).astype(BF16)

    gates = _dot(h, w1_ref[:, COL_GATE:COL_END])
    gc_ref[0] = jax.nn.sigmoid(gates[:, :D_MODEL])
    ga_ref[0] = jax.nn.sigmoid(gates[:, D_MODEL:])


def _inproj(x, mod, g1, w1, qg, wq, kvg, wk, ct, sa, sb, conv_w, conv_b, ln_g, ln_b, *, tm, tk, conv_rows=32):
    b, s, _ = x.shape
    per_chunk = tk // tm
    nh = tm // CONV_HALO
    last_halo = s // CONV_HALO - 1
    tok = lambda width: pl.BlockSpec((1, tm, width), lambda bi, i: (bi, i, 0))
    head = pl.BlockSpec((1, N_HEADS, tm, HEAD_DIM), lambda bi, i: (bi, 0, i, 0))
    head_t = pl.BlockSpec((1, N_HEADS, 1, HEAD_DIM, tm), lambda bi, i: (bi, 0, i // per_chunk, 0, i % per_chunk))
    tab = pl.BlockSpec((tm, LANES), lambda bi, i: (i, 0))
    return pl.pallas_call(
        functools.partial(_inproj_kernel, conv_rows=conv_rows),
        grid=(b, s // tm),
        in_specs=[
            tok(D_MODEL),
            pl.BlockSpec((1, CONV_HALO, D_MODEL), lambda bi, i: (bi, jnp.maximum(i * nh - 1, 0), 0)),
            pl.BlockSpec((1, CONV_HALO, D_MODEL), lambda bi, i: (bi, jnp.minimum((i + 1) * nh, last_halo), 0)),
            pl.BlockSpec((1, N_MOD, D_MODEL), lambda bi, i: (bi, 0, 0)),
            _const_spec((1, D_MODEL)),
            _const_spec((D_MODEL, COL_END)),
            _const_spec((1, Q_LORA_RANK)),
            _const_spec((Q_LORA_RANK, N_HEADS * HEAD_DIM)),
            _const_spec((1, KV_LORA_RANK)),
            _const_spec((KV_LORA_RANK, N_HEADS * HEAD_DIM)),
            tab, tab, tab,
            _const_spec((CONV_KERNEL, CONV_WIDTH)),
            _const_spec((1, CONV_WIDTH)),
            _const_spec((1, CONV_WIDTH)),
            _const_spec((1, CONV_WIDTH)),
        ],
        out_specs=[tok(CONV_WIDTH), head, head, head_t, tok(D_MODEL), tok(D_MODEL)],
        out_shape=[
            jax.ShapeDtypeStruct((b, s, CONV_WIDTH), BF16),
            jax.ShapeDtypeStruct((b, N_HEADS, s, HEAD_DIM), BF16),
            jax.ShapeDtypeStruct((b, N_HEADS, s, HEAD_DIM), BF16),
            jax.ShapeDtypeStruct((b, N_HEADS, s // tk, HEAD_DIM, tk), BF16),
            jax.ShapeDtypeStruct((b, s, D_MODEL), F32),
            jax.ShapeDtypeStruct((b, s, D_MODEL), F32),
        ],
        scratch_shapes=[
            pltpu.VMEM((tm + 2 * CONV_HALO, CONV_WIDTH), F32),
            pltpu.VMEM((SUBLANES - 1, tm + 2 * CONV_HALO - SUBLANES, CONV_WIDTH), F32),
        ],
        compiler_params=pltpu.CompilerParams(
            dimension_semantics=("arbitrary", "arbitrary"), vmem_limit_bytes=VMEM_LIMIT_BYTES),
        name="in_proj",
    )(x, x, x, mod, g1, w1, qg, wq, kvg, wk, ct, sa, sb, conv_w, conv_b, ln_g, ln_b)


def _attn_kernel(q_ref, k_ref, vt_ref, o_ref, s0_ref, s1_ref):
    tq = q_ref.shape[2]
    n_chunks, _, tk = vt_ref.shape[2:]
    qs = [q_ref[0, hh] for hh in range(2)]

    def scores(s_ref, hh, c):
        kc = k_ref[0, hh, pl.ds(pl.multiple_of(c * tk, tk), tk), :]
        s_ref[hh] = lax.dot_general(kc, qs[hh], (((1,), (1,)), ((), ())), preferred_element_type=F32)

    def update(s_ref, hh, c, state):
        m, acc = state
        st = s_ref[hh]
        m_new = jnp.maximum(m, jnp.max(st, axis=0, keepdims=True))
        alpha = jnp.exp2(m - m_new)
        pt = jnp.exp2(st - m_new).astype(BF16)
        return m_new, alpha * acc + _dot(vt_ref[0, hh, c], pt)

    def step(c, cur_ref, nxt_ref, state, with_next=True):
        new = []
        for hh in range(2):
            if with_next:
                scores(nxt_ref, hh, c + 1)
            new.append(update(cur_ref, hh, c, state[hh]))
        return tuple(new)

    def pairs(c0, state, n_pairs, last=False):
        for p in range(n_pairs):
            state = step(c0 + 2 * p, s0_ref, s1_ref, state)
            state = step(c0 + 2 * p + 1, s1_ref, s0_ref, state, with_next=not (last and p == n_pairs - 1))
        return state

    per_iter = 2 if n_chunks % 4 == 0 else 1
    for hh in range(2):
        scores(s0_ref, hh, 0)
    state = tuple((jnp.full((1, tq), -jnp.inf, F32), jnp.zeros((HEAD_DIM, tq), F32)) for _ in range(2))
    n_iter = n_chunks // (2 * per_iter) - 1
    state = lax.fori_loop(0, n_iter, lambda j, st: pairs(j * (2 * per_iter), st, per_iter), state)
    state = pairs(n_iter * 2 * per_iter, state, per_iter, last=True)
    ot = jnp.concatenate([acc[:V_HEAD_DIM] / acc[V_HEAD_DIM:V_HEAD_DIM + 1] for _, acc in state], axis=0)
    o_ref[0] = ot.T.astype(BF16)


def _attention(q, k, vt, *, tq):
    b, _, s, _ = q.shape
    n_chunks, _, tk = vt.shape[2:]
    return pl.pallas_call(
        _attn_kernel,
        grid=(b, N_HEADS // 2, s // tq),
        in_specs=[
            pl.BlockSpec((1, 2, tq, HEAD_DIM), lambda bi, hp, i: (bi, hp, i, 0)),
            pl.BlockSpec((1, 2, s, HEAD_DIM), lambda bi, hp, i: (bi, hp, 0, 0)),
            pl.BlockSpec((1, 2, n_chunks, HEAD_DIM, tk), lambda bi, hp, i: (bi, hp, 0, 0, 0)),
        ],
        out_specs=pl.BlockSpec((1, tq, LANES), lambda bi, hp, i: (bi, i, hp)),
        out_shape=jax.ShapeDtypeStruct((b, s, N_HEADS * V_HEAD_DIM), BF16),
        scratch_shapes=[pltpu.VMEM((2, tk, tq), F32), pltpu.VMEM((2, tk, tq), F32)],
        compiler_params=pltpu.CompilerParams(
            dimension_semantics=("arbitrary", "arbitrary", "arbitrary"), vmem_limit_bytes=VMEM_LIMIT_BYTES),
        name="attention",
    )(q, k, vt)


def _outproj_kernel(x_ref, mod_ref, o_ref, z_ref, gc_ref, ga_ref, wao_ref, wco_ref, wout_ref, g2_ref,
                    wup_ref, wdn_ref, fg_ref, y_ref, *, final):
    x = x_ref[0]
    mod = mod_ref[0]
    y_attn = _dot(o_ref[0], wao_ref[...])
    y_conv = _dot(z_ref[0], wco_ref[...])
    mix_in = (gc_ref[0] * y_conv + ga_ref[0] * y_attn).astype(BF16)
    x1 = x + mod[2:3] * _dot(mix_in, wout_ref[...])
    h2 = (_rms(x1, g2_ref[...]) * (1.0 + mod[4:5]) + mod[3:4]).astype(BF16)
    acc = jnp.zeros(x.shape, F32)
    for c in range(0, D_FF, FF_CHUNK):
        up = _dot(h2, wup_ref[:, c:c + FF_CHUNK])
        acc = acc + _dot(jnp.square(jnp.maximum(up, 0.0)).astype(BF16), wdn_ref[c:c + FF_CHUNK, :])
    x2 = x1 + mod[5:6] * acc
    if final:
        x2 = _rms(x2, fg_ref[...])
    y_ref[0] = x2


def _outproj(x, mod, o, z, gc, ga, wao, wco, wout, g2, wup, wdn, fg, *, tm, final):
    b, s, _ = x.shape
    tok = lambda width: pl.BlockSpec((1, tm, width), lambda bi, i: (bi, i, 0))
    return pl.pallas_call(
        functools.partial(_outproj_kernel, final=final),
        grid=(b, s // tm),
        in_specs=[
            tok(D_MODEL),
            pl.BlockSpec((1, N_MOD, D_MODEL), lambda bi, i: (bi, 0, 0)),
            tok(N_HEADS * V_HEAD_DIM), tok(CONV_WIDTH), tok(D_MODEL), tok(D_MODEL),
            _const_spec((N_HEADS * V_HEAD_DIM, D_MODEL)),
            _const_spec((CONV_WIDTH, D_MODEL)),
            _const_spec((D_MODEL, D_MODEL)),
            _const_spec((1, D_MODEL)),
            _const_spec((D_MODEL, D_FF)),
            _const_spec((D_FF, D_MODEL)),
            _const_spec((1, D_MODEL)),
        ],
        out_specs=tok(D_MODEL),
        out_shape=jax.ShapeDtypeStruct((b, s, D_MODEL), F32),
        compiler_params=pltpu.CompilerParams(
            dimension_semantics=("arbitrary", "arbitrary"), vmem_limit_bytes=VMEM_LIMIT_BYTES),
        name="out_proj_mlp",
    )(x, mod, o, z, gc, ga, wao, wco, wout, g2, wup, wdn, fg)


def _rope_tables(seq_len):
    inv = ROPE_BASE ** (-jnp.arange(0, QK_ROPE_DIM, 2, dtype=F32) / QK_ROPE_DIM)
    ang = jnp.arange(seq_len, dtype=F32)[:, None] * inv[None, :]
    cos, sin = jnp.cos(ang), jnp.sin(ang)
    one = jnp.ones((seq_len, QK_NOPE_DIM), F32)
    zero = jnp.zeros((seq_len, QK_NOPE_DIM), F32)
    half = jnp.zeros_like(sin)
    ct = jnp.concatenate([one, cos, cos], axis=-1)
    sa = jnp.concatenate([zero, -sin, half], axis=-1)
    sb = jnp.concatenate([zero, half, sin], axis=-1)
    return ct, sa, sb


def _pack_w_in(w_in):
    c0 = 2 * CONV_WIDTH
    c1 = c0 + Q_LORA_RANK
    c2 = c1 + KV_LORA_RANK
    c3 = c2 + QK_ROPE_DIM
    pad = jnp.zeros((w_in.shape[0], LANES - QK_ROPE_DIM), w_in.dtype)
    return jnp.concatenate([w_in[:, :c2], pad, w_in[:, c2:c3], w_in[:, c3:]], axis=-1).astype(BF16)


def _trunk(x, mods, layer_weights, final_g, *, tm, tq, tk):
    ct, sa, sb = _rope_tables(x.shape[1])
    n_layers = len(layer_weights)
    for l, w in enumerate(layer_weights):
        mod = mods[l]
        z, q, k, vt, gc, ga = _inproj(x, mod, w["g1"], w["w1"], w["qg"], w["wq"], w["kvg"], w["wk"], ct, sa, sb,
                                      w["conv_w"], w["conv_b"], w["ln_g"], w["ln_b"], tm=tm, tk=tk)
        o = _attention(q, k, vt, tq=tq)
        x = _outproj(x, mod, o, z, gc, ga, w["wao"], w["wco"], w["wout"], w["g2"], w["wup"], w["wdn"],
                     final_g, tm=tm, final=(l == n_layers - 1))
    return x


def kernel(x_prompt, x_sample, c_prompt, c_sample, ada_w, ada_b, norm_mix_g, w_in, q_norm_g, w_q_up, kv_norm_g, w_kv_up, w_attn_o, conv_dw, conv_dw_b, conv_ln_g, conv_ln_b, w_conv_out, w_out, norm_mlp_g, w_mlp_up, w_mlp_down, final_g):
    n_layers = ada_w.shape[0]
    bp, bs = c_prompt.shape[0], c_sample.shape[0]
    pad_rows = -(bp + bs) % 8
    c_all = jnp.concatenate([c_prompt, c_sample, jnp.zeros((pad_rows, D_MODEL), F32)], axis=0)
    mods = _ada(c_all, ada_w, ada_b)
    mods_p = mods[:, :bp].reshape(n_layers, bp, N_MOD, D_MODEL)
    mods_s = mods[:, bp:bp + bs].reshape(n_layers, bs, N_MOD, D_MODEL)

    row = lambda a: a.reshape(1, -1)
    layer_weights = []
    for l in range(n_layers):
        layer_weights.append(dict(
            g1=row(norm_mix_g[l]), w1=_pack_w_in(w_in[l]), qg=row(q_norm_g[l]), wq=w_q_up[l].astype(BF16),
            kvg=row(kv_norm_g[l]), wk=w_kv_up[l].astype(BF16),
            conv_w=conv_dw[l], conv_b=row(conv_dw_b[l]), ln_g=row(conv_ln_g[l]), ln_b=row(conv_ln_b[l]),
            wao=w_attn_o[l].astype(BF16), wco=w_conv_out[l].astype(BF16), wout=w_out[l].astype(BF16),
            g2=row(norm_mlp_g[l]), wup=w_mlp_up[l].astype(BF16), wdn=w_mlp_down[l].astype(BF16)))
    fg = row(final_g)
    tiles = dict(tm=512, tk=512)
    y_prompt = _trunk(x_prompt, mods_p, layer_weights, fg, tq=1024, **tiles)
    y_sample = _trunk(x_sample, mods_s, layer_weights, fg, tq=512, **tiles)
    return (y_prompt, y_sample)
```

```python
import functools

import jax
import jax.numpy as jnp
from jax import lax
from jax.experimental import pallas as pl
from jax.experimental.pallas import tpu as pltpu

D_MODEL = 1024
N_HEADS = 8
QK_NOPE_DIM = 64
QK_ROPE_DIM = 64
V_HEAD_DIM = 64
HEAD_DIM = QK_NOPE_DIM + QK_ROPE_DIM
Q_LORA_RANK = 384
KV_LORA_RANK = 256
CONV_WIDTH = 512
CONV_KERNEL = 31
CONV_HALO = 16
D_FF = 4 * D_MODEL
FF_CHUNK = 1024
NORM_EPS = 1e-6
QK_SCALE_LOG2E = (QK_NOPE_DIM + QK_ROPE_DIM) ** -0.5 * 1.4426950408889634
N_MOD = 6
ROPE_BASE = 10000.0
LANES = 128
SUBLANES = 8
VMEM_LIMIT_BYTES = 56 * 1024 * 1024

COL_CONV = 0
COL_QA = COL_CONV + 2 * CONV_WIDTH
COL_KVA = COL_QA + Q_LORA_RANK
COL_KR = COL_KVA + KV_LORA_RANK
COL_GATE = COL_KR + LANES
COL_END = COL_GATE + 2 * D_MODEL

F32 = jnp.float32
BF16 = jnp.bfloat16


def _rms(x, g):
    return x * lax.rsqrt(jnp.mean(x * x, axis=-1, keepdims=True) + NORM_EPS) * g


def _dot(a, b):
    return jnp.dot(a, b, preferred_element_type=F32)


def _const_spec(shape):
    return pl.BlockSpec(shape, lambda *_: (0,) * len(shape), pipeline_mode=pl.Buffered(1))


def _rope(t, ct, sa, sb):
    return t * ct + pltpu.roll(t, LANES - QK_ROPE_DIM // 2, 1) * sa + pltpu.roll(t, QK_ROPE_DIM // 2, 1) * sb


def _ada_kernel(c_ref, w_ref, b_ref, o_ref):
    c = c_ref[...]
    sc = (c * jax.nn.sigmoid(c)).astype(BF16)
    o_ref[0] = _dot(sc, w_ref[0].astype(BF16)) + b_ref[0]


def _ada(c_all, ada_w, ada_b):
    n_layers = ada_w.shape[0]
    bp = c_all.shape[0]
    return pl.pallas_call(
        _ada_kernel,
        grid=(n_layers, N_MOD),
        in_specs=[
            pl.BlockSpec((bp, D_MODEL), lambda l, j: (0, 0)),
            pl.BlockSpec((1, D_MODEL, D_MODEL), lambda l, j: (l, 0, j)),
            pl.BlockSpec((1, 1, D_MODEL), lambda l, j: (l, 0, j)),
        ],
        out_specs=pl.BlockSpec((1, bp, D_MODEL), lambda l, j: (l, 0, j)),
        out_shape=jax.ShapeDtypeStruct((n_layers, bp, N_MOD * D_MODEL), F32),
        compiler_params=pltpu.CompilerParams(dimension_semantics=("arbitrary", "arbitrary")),
        name="ada_mod",
    )(c_all, ada_w, ada_b.reshape(n_layers, 1, N_MOD * D_MODEL))


def _conv_branch(ext_ref, sh_ref, w_ref, b_ref, lg_ref, lb_ref, z_ref, *, rows):
    tm = z_ref.shape[1]
    n_sh = sh_ref.shape[1]
    for s in range(1, SUBLANES):
        sh_ref[s - 1] = ext_ref[s:s + n_sh, :]
    w = w_ref[...]
    off = CONV_HALO - CONV_KERNEL // 2
    for r0 in range(0, tm, rows):
        acc = jnp.broadcast_to(b_ref[...], (rows, CONV_WIDTH))
        for k in range(CONV_KERNEL):
            base, s = divmod(k + off, SUBLANES)
            lo = r0 + base * SUBLANES
            tap = ext_ref[lo:lo + rows, :] if s == 0 else sh_ref[s - 1, lo:lo + rows, :]
            acc = acc + w[k:k + 1] * tap
        mu = jnp.mean(acc, axis=-1, keepdims=True)
        xc = acc - mu
        y = xc * lax.rsqrt(jnp.mean(xc * xc, axis=-1, keepdims=True) + NORM_EPS) * lg_ref[...] + lb_ref[...]
        z_ref[0, r0:r0 + rows, :] = (y * jax.nn.sigmoid(y)).astype(BF16)


def _inproj_kernel(x_ref, xp_ref, xn_ref, mod_ref, g1_ref, w1_ref, qg_ref, wq_ref, kvg_ref, wk_ref,
                   ct_ref, sa_ref, sb_ref, cw_ref, cb_ref, lg_ref, lb_ref,
                   z_ref, q_ref, k_ref, vt_ref, gc_ref, ga_ref, ext_ref, sh_ref, *, conv_rows):
    i = pl.program_id(1)
    tm = x_ref.shape[1]
    mod = mod_ref[0]
    norm_mod = lambda x: (_rms(x, g1_ref[...]) * (1.0 + mod[1:2]) + mod[0:1]).astype(BF16)
    h_ext = jnp.concatenate([norm_mod(xp_ref[0]), norm_mod(x_ref[0]), norm_mod(xn_ref[0])], axis=0)
    h = h_ext[CONV_HALO:CONV_HALO + tm]
    ct, sa, sb = ct_ref[...], sa_ref[...], sb_ref[...]

    conv_in = _dot(h_ext, w1_ref[:, COL_CONV:COL_QA])
    u = conv_in[:, :CONV_WIDTH] * jax.nn.sigmoid(conv_in[:, CONV_WIDTH:])
    row = lax.broadcasted_iota(jnp.int32, (tm + 2 * CONV_HALO, 1), 0)
    inside = jnp.logical_and(jnp.logical_or(row >= CONV_HALO, i > 0),
                             jnp.logical_or(row < CONV_HALO + tm, i < pl.num_programs(1) - 1))
    ext_ref[...] = jnp.where(inside, u, 0.0)
    _conv_branch(ext_ref, sh_ref, cw_ref, cb_ref, lg_ref, lb_ref, z_ref, rows=conv_rows)

    q_a = _dot(h, w1_ref[:, COL_QA:COL_KVA])
    q = _dot(_rms(q_a, qg_ref[...]).astype(BF16), wq_ref[...])
    for hd in range(N_HEADS):
        qh = _rope(q[:, hd * HEAD_DIM:(hd + 1) * HEAD_DIM], ct, sa, sb)
        q_ref[0, hd] = (qh * QK_SCALE_LOG2E).astype(BF16)

    kv_a = _dot(h, w1_ref[:, COL_KVA:COL_KR])
    kv_n = _rms(kv_a, kvg_ref[...]).astype(BF16)
    k_rope = _rope(_dot(h, w1_ref[:, COL_KR:COL_GATE]), ct, sa, sb)
    kv = _dot(kv_n, wk_ref[...])
    lane = lax.broadcasted_iota(jnp.int32, k_rope.shape, 1)
    ones = jnp.ones((HEAD_DIM - V_HEAD_DIM, tm), F32)
    for hd in range(N_HEADS):
        kv_h = kv[:, hd * HEAD_DIM:(hd + 1) * HEAD_DIM]
        k_ref[0, hd] = jnp.where(lane < QK_NOPE_DIM, kv_h, k_rope).astype(BF16)
        vt_ref[0, hd, 0] = jnp.concatenate([kv_h.T[QK_NOPE_DIM:], ones], axis=0).astype(BF16)

    gates = _dot(h, w1_ref[:, COL_GATE:COL_END])
    gc_ref[0] = gates[:, :D_MODEL].astype(BF16)
    ga_ref[0] = gates[:, D_MODEL:].astype(BF16)


def _inproj(x, mod, g1, w1, qg, wq, kvg, wk, ct, sa, sb, conv_w, conv_b, ln_g, ln_b, *, tm, tk, conv_rows=32):
    b, s, _ = x.shape
    per_chunk = tk // tm
    nh = tm // CONV_HALO
    last_halo = s // CONV_HALO - 1
    tok = lambda width: pl.BlockSpec((1, tm, width), lambda bi, i: (bi, i, 0))
    head = pl.BlockSpec((1, N_HEADS, tm, HEAD_DIM), lambda bi, i: (bi, 0, i, 0))
    head_t = pl.BlockSpec((1, N_HEADS, 1, HEAD_DIM, tm), lambda bi, i: (bi, 0, i // per_chunk, 0, i % per_chunk))
    tab = pl.BlockSpec((tm, LANES), lambda bi, i: (i, 0))
    return pl.pallas_call(
        functools.partial(_inproj_kernel, conv_rows=conv_rows),
        grid=(b, s // tm),
        in_specs=[
            tok(D_MODEL),
            pl.BlockSpec((1, CONV_HALO, D_MODEL), lambda bi, i: (bi, jnp.maximum(i * nh - 1, 0), 0)),
            pl.BlockSpec((1, CONV_HALO, D_MODEL), lambda bi, i: (bi, jnp.minimum((i + 1) * nh, last_halo), 0)),
            pl.BlockSpec((1, N_MOD, D_MODEL), lambda bi, i: (bi, 0, 0)),
            _const_spec((1, D_MODEL)),
            _const_spec((D_MODEL, COL_END)),
            _const_spec((1, Q_LORA_RANK)),
            _const_spec((Q_LORA_RANK, N_HEADS * HEAD_DIM)),
            _const_spec((1, KV_LORA_RANK)),
            _const_spec((KV_LORA_RANK, N_HEADS * HEAD_DIM)),
            tab, tab, tab,
            _const_spec((CONV_KERNEL, CONV_WIDTH)),
            _const_spec((1, CONV_WIDTH)),
            _const_spec((1, CONV_WIDTH)),
            _const_spec((1, CONV_WIDTH)),
        ],
        out_specs=[tok(CONV_WIDTH), head, head, head_t, tok(D_MODEL), tok(D_MODEL)],
        out_shape=[
            jax.ShapeDtypeStruct((b, s, CONV_WIDTH), BF16),
            jax.ShapeDtypeStruct((b, N_HEADS, s, HEAD_DIM), BF16),
            jax.ShapeDtypeStruct((b, N_HEADS, s, HEAD_DIM), BF16),
            jax.ShapeDtypeStruct((b, N_HEADS, s // tk, HEAD_DIM, tk), BF16),
            jax.ShapeDtypeStruct((b, s, D_MODEL), BF16),
            jax.ShapeDtypeStruct((b, s, D_MODEL), BF16),
        ],
        scratch_shapes=[
            pltpu.VMEM((tm + 2 * CONV_HALO, CONV_WIDTH), F32),
            pltpu.VMEM((SUBLANES - 1, tm + 2 * CONV_HALO - SUBLANES, CONV_WIDTH), F32),
        ],
        compiler_params=pltpu.CompilerParams(
            dimension_semantics=("arbitrary", "arbitrary"), vmem_limit_bytes=VMEM_LIMIT_BYTES),
        name="in_proj",
    )(x, x, x, mod, g1, w1, qg, wq, kvg, wk, ct, sa, sb, conv_w, conv_b, ln_g, ln_b)


def _attn_kernel(q_ref, k_ref, vt_ref, o_ref, s0_ref, s1_ref):
    tq = s0_ref.shape[2]
    n_chunks, _, tk = vt_ref.shape[2:]
    n_q = q_ref.shape[2] // tq

    def scores(s_ref, hh, qi, c):
        kc = k_ref[0, hh, pl.ds(pl.multiple_of(c * tk, tk), tk), :]
        qt = q_ref[0, hh, pl.ds(pl.multiple_of(qi * tq, tq), tq), :]
        s_ref[hh] = lax.dot_general(kc, qt, (((1,), (1,)), ((), ())), preferred_element_type=F32)

    def update(s_ref, hh, c, state):
        m, acc = state
        st = s_ref[hh]
        m_new = jnp.maximum(m, jnp.max(st, axis=0, keepdims=True))
        alpha = jnp.exp2(m - m_new)
        pt = jnp.exp2(st - m_new).astype(BF16)
        return m_new, alpha * acc + _dot(vt_ref[0, hh, c], pt)

    def step(c, cur_ref, nxt_ref, state, nxt):
        new = []
        for hh in range(2):
            if nxt is not None:
                scores(nxt_ref, hh, *nxt)
            new.append(update(cur_ref, hh, c, state[hh]))
        return tuple(new)

    def pairs(qi, c0, state, n_pairs, after=False):
        for p in range(n_pairs):
            c = c0 + 2 * p
            ends = after is not False and p == n_pairs - 1
            state = step(c, s0_ref, s1_ref, state, (qi, c + 1))
            state = step(c + 1, s1_ref, s0_ref, state, after if ends else (qi, c + 2))
        return state

    per_iter = 2 if n_chunks % 4 == 0 else 1
    n_iter = n_chunks // (2 * per_iter) - 1

    def q_tile(qi, is_last):
        state = tuple((jnp.full((1, tq), -jnp.inf, F32), jnp.zeros((HEAD_DIM, tq), F32)) for _ in range(2))
        state = lax.fori_loop(0, n_iter, lambda j, st: pairs(qi, j * (2 * per_iter), st, per_iter), state)
        state = pairs(qi, n_iter * 2 * per_iter, state, per_iter, after=None if is_last else (qi + 1, 0))
        ot = jnp.concatenate([acc[:V_HEAD_DIM] / acc[V_HEAD_DIM:V_HEAD_DIM + 1] for _, acc in state], axis=0)
        o_ref[0, pl.ds(pl.multiple_of(qi * tq, tq), tq), :] = ot.T.astype(BF16)

    for hh in range(2):
        scores(s0_ref, hh, 0, 0)

    def body(qi, carry):
        q_tile(qi, False)
        return carry

    lax.fori_loop(0, n_q - 1, body, 0)
    q_tile(n_q - 1, True)


def _attention(q, k, vt, *, tq):
    b, _, s, _ = q.shape
    n_chunks, _, tk = vt.shape[2:]
    return pl.pallas_call(
        _attn_kernel,
        grid=(b, N_HEADS // 2),
        in_specs=[
            pl.BlockSpec((1, 2, s, HEAD_DIM), lambda bi, hp: (bi, hp, 0, 0)),
            pl.BlockSpec((1, 2, s, HEAD_DIM), lambda bi, hp: (bi, hp, 0, 0)),
            pl.BlockSpec((1, 2, n_chunks, HEAD_DIM, tk), lambda bi, hp: (bi, hp, 0, 0, 0)),
        ],
        out_specs=pl.BlockSpec((1, s, LANES), lambda bi, hp: (bi, 0, hp)),
        out_shape=jax.ShapeDtypeStruct((b, s, N_HEADS * V_HEAD_DIM), BF16),
        scratch_shapes=[pltpu.VMEM((2, tk, tq), F32), pltpu.VMEM((2, tk, tq), F32)],
        compiler_params=pltpu.CompilerParams(
            dimension_semantics=("arbitrary", "arbitrary"), vmem_limit_bytes=VMEM_LIMIT_BYTES),
        name="attention",
    )(q, k, vt)


def _outproj_kernel(x_ref, mod_ref, o_ref, z_ref, gc_ref, ga_ref, wao_ref, wco_ref, wout_ref, g2_ref,
                    wup_ref, wdn_ref, fg_ref, y_ref, *, final):
    x = x_ref[0]
    mod = mod_ref[0]
    y_attn = _dot(o_ref[0], wao_ref[...])
    y_conv = _dot(z_ref[0], wco_ref[...])
    g_conv = jax.nn.sigmoid(gc_ref[0].astype(F32))
    g_attn = jax.nn.sigmoid(ga_ref[0].astype(F32))
    mix_in = (g_conv * y_conv + g_attn * y_attn).astype(BF16)
    x1 = x + mod[2:3] * _dot(mix_in, wout_ref[...])
    h2 = (_rms(x1, g2_ref[...]) * (1.0 + mod[4:5]) + mod[3:4]).astype(BF16)
    acc = jnp.zeros(x.shape, F32)
    for c in range(0, D_FF, FF_CHUNK):
        up = _dot(h2, wup_ref[:, c:c + FF_CHUNK])
        acc = acc + _dot(jnp.square(jnp.maximum(up, 0.0)).astype(BF16), wdn_ref[c:c + FF_CHUNK, :])
    x2 = x1 + mod[5:6] * acc
    if final:
        x2 = _rms(x2, fg_ref[...])
    y_ref[0] = x2


def _outproj(x, mod, o, z, gc, ga, wao, wco, wout, g2, wup, wdn, fg, *, tm, final):
    b, s, _ = x.shape
    tok = lambda width: pl.BlockSpec((1, tm, width), lambda bi, i: (bi, i, 0))
    return pl.pallas_call(
        functools.partial(_outproj_kernel, final=final),
        grid=(b, s // tm),
        in_specs=[
            tok(D_MODEL),
            pl.BlockSpec((1, N_MOD, D_MODEL), lambda bi, i: (bi, 0, 0)),
            tok(N_HEADS * V_HEAD_DIM), tok(CONV_WIDTH), tok(D_MODEL), tok(D_MODEL),
            _const_spec((N_HEADS * V_HEAD_DIM, D_MODEL)),
            _const_spec((CONV_WIDTH, D_MODEL)),
            _const_spec((D_MODEL, D_MODEL)),
            _const_spec((1, D_MODEL)),
            _const_spec((D_MODEL, D_FF)),
            _const_spec((D_FF, D_MODEL)),
            _const_spec((1, D_MODEL)),
        ],
        out_specs=tok(D_MODEL),
        out_shape=jax.ShapeDtypeStruct((b, s, D_MODEL), F32),
        compiler_params=pltpu.CompilerParams(
            dimension_semantics=("arbitrary", "arbitrary"), vmem_limit_bytes=VMEM_LIMIT_BYTES),
        name="out_proj_mlp",
    )(x, mod, o, z, gc, ga, wao, wco, wout, g2, wup, wdn, fg)


def _rope_tables(seq_len):
    inv = ROPE_BASE ** (-jnp.arange(0, QK_ROPE_DIM, 2, dtype=F32) / QK_ROPE_DIM)
    ang = jnp.arange(seq_len, dtype=F32)[:, None] * inv[None, :]
    cos, sin = jnp.cos(ang), jnp.sin(ang)
    one = jnp.ones((seq_len, QK_NOPE_DIM), F32)
    zero = jnp.zeros((seq_len, QK_NOPE_DIM), F32)
    half = jnp.zeros_like(sin)
    ct = jnp.concatenate([one, cos, cos], axis=-1)
    sa = jnp.concatenate([zero, -sin, half], axis=-1)
    sb = jnp.concatenate([zero, half, sin], axis=-1)
    return ct, sa, sb


def _pack_w_in(w_in):
    c0 = 2 * CONV_WIDTH
    c1 = c0 + Q_LORA_RANK
    c2 = c1 + KV_LORA_RANK
    c3 = c2 + QK_ROPE_DIM
    pad = jnp.zeros((w_in.shape[0], LANES - QK_ROPE_DIM), w_in.dtype)
    return jnp.concatenate([w_in[:, :c2], pad, w_in[:, c2:c3], w_in[:, c3:]], axis=-1).astype(BF16)


def _trunk(x, mods, layer_weights, final_g, *, tm, tq, tk):
    ct, sa, sb = _rope_tables(x.shape[1])
    n_layers = len(layer_weights)
    for l, w in enumerate(layer_weights):
        mod = mods[l]
        z, q, k, vt, gc, ga = _inproj(x, mod, w["g1"], w["w1"], w["qg"], w["wq"], w["kvg"], w["wk"], ct, sa, sb,
                                      w["conv_w"], w["conv_b"], w["ln_g"], w["ln_b"], tm=tm, tk=tk)
        o = _attention(q, k, vt, tq=tq)
        x = _outproj(x, mod, o, z, gc, ga, w["wao"], w["wco"], w["wout"], w["g2"], w["wup"], w["wdn"],
                     final_g, tm=tm, final=(l == n_layers - 1))
    return x


def kernel(x_prompt, x_sample, c_prompt, c_sample, ada_w, ada_b, norm_mix_g, w_in, q_norm_g, w_q_up, kv_norm_g, w_kv_up, w_attn_o, conv_dw, conv_dw_b, conv_ln_g, conv_ln_b, w_conv_out, w_out, norm_mlp_g, w_mlp_up, w_mlp_down, final_g):
    n_layers = ada_w.shape[0]
    bp, bs = c_prompt.shape[0], c_sample.shape[0]
    pad_rows = -(bp + bs) % 8
    c_all = jnp.concatenate([c_prompt, c_sample, jnp.zeros((pad_rows, D_MODEL), F32)], axis=0)
    mods = _ada(c_all, ada_w, ada_b)
    mods_p = mods[:, :bp].reshape(n_layers, bp, N_MOD, D_MODEL)
    mods_s = mods[:, bp:bp + bs].reshape(n_layers, bs, N_MOD, D_MODEL)

    row = lambda a: a.reshape(1, -1)
    layer_weights = []
    for l in range(n_layers):
        layer_weights.append(dict(
            g1=row(norm_mix_g[l]), w1=_pack_w_in(w_in[l]), qg=row(q_norm_g[l]), wq=w_q_up[l].astype(BF16),
            kvg=row(kv_norm_g[l]), wk=w_kv_up[l].astype(BF16),
            conv_w=conv_dw[l], conv_b=row(conv_dw_b[l]), ln_g=row(conv_ln_g[l]), ln_b=row(conv_ln_b[l]),
            wao=w_attn_o[l].astype(BF16), wco=w_conv_out[l].astype(BF16), wout=w_out[l].astype(BF16),
            g2=row(norm_mlp_g[l]), wup=w_mlp_up[l].astype(BF16), wdn=w_mlp_down[l].astype(BF16)))
    fg = row(final_g)
    tiles = dict(tm=512, tk=512, tq=512)
    y_prompt = _trunk(x_prompt, mods_p, layer_weights, fg, **tiles)
    y_sample = _trunk(x_sample, mods_s, layer_weights, fg, **tiles)
    return (y_prompt, y_sample)
```

```python
import functools

import jax
import jax.numpy as jnp
from jax import lax
from jax.experimental import pallas as pl
from jax.experimental.pallas import tpu as pltpu

D_MODEL = 1024
N_HEADS = 8
QK_NOPE_DIM = 64
QK_ROPE_DIM = 64
V_HEAD_DIM = 64
HEAD_DIM = QK_NOPE_DIM + QK_ROPE_DIM
Q_LORA_RANK = 384
KV_LORA_RANK = 256
CONV_WIDTH = 512
CONV_KERNEL = 31
CONV_HALO = 16
D_FF = 4 * D_MODEL
FF_CHUNK = 1024
NORM_EPS = 1e-6
QK_SCALE_LOG2E = (QK_NOPE_DIM + QK_ROPE_DIM) ** -0.5 * 1.4426950408889634
N_MOD = 6
ROPE_BASE = 10000.0
LANES = 128
SUBLANES = 8
VMEM_LIMIT_BYTES = 56 * 1024 * 1024

COL_CONV = 0
COL_QA = COL_CONV + 2 * CONV_WIDTH
COL_KVA = COL_QA + Q_LORA_RANK
COL_KR = COL_KVA + KV_LORA_RANK
COL_GATE = COL_KR + LANES
COL_END = COL_GATE + 2 * D_MODEL

F32 = jnp.float32
BF16 = jnp.bfloat16


def _rms(x, g):
    return x * lax.rsqrt(jnp.mean(x * x, axis=-1, keepdims=True) + NORM_EPS) * g


def _dot(a, b):
    return jnp.dot(a, b, preferred_element_type=F32)


def _const_spec(shape):
    return pl.BlockSpec(shape, lambda *_: (0,) * len(shape), pipeline_mode=pl.Buffered(1))


def _rope(t, ct, sa, sb):
    return t * ct + pltpu.roll(t, LANES - QK_ROPE_DIM // 2, 1) * sa + pltpu.roll(t, QK_ROPE_DIM // 2, 1) * sb


def _ada_kernel(c_ref, w_ref, b_ref, o_ref):
    c = c_ref[...]
    sc = (c * jax.nn.sigmoid(c)).astype(BF16)
    o_ref[0] = _dot(sc, w_ref[0].astype(BF16)) + b_ref[0]


def _ada(c_all, ada_w, ada_b):
    n_layers = ada_w.shape[0]
    bp = c_all.shape[0]
    return pl.pallas_call(
        _ada_kernel,
        grid=(n_layers, N_MOD),
        in_specs=[
            pl.BlockSpec((bp, D_MODEL), lambda l, j: (0, 0)),
            pl.BlockSpec((1, D_MODEL, D_MODEL), lambda l, j: (l, 0, j)),
            pl.BlockSpec((1, 1, D_MODEL), lambda l, j: (l, 0, j)),
        ],
        out_specs=pl.BlockSpec((1, bp, D_MODEL), lambda l, j: (l, 0, j)),
        out_shape=jax.ShapeDtypeStruct((n_layers, bp, N_MOD * D_MODEL), F32),
        compiler_params=pltpu.CompilerParams(dimension_semantics=("arbitrary", "arbitrary")),
        name="ada_mod",
    )(c_all, ada_w, ada_b.reshape(n_layers, 1, N_MOD * D_MODEL))


def _conv_branch(ext_ref, sh_ref, w_ref, b_ref, lg_ref, lb_ref, z_ref, *, rows):
    tm = z_ref.shape[1]
    n_sh = sh_ref.shape[1]
    for s in range(1, SUBLANES):
        sh_ref[s - 1] = ext_ref[s:s + n_sh, :]
    w = w_ref[...]
    off = CONV_HALO - CONV_KERNEL // 2
    for r0 in range(0, tm, rows):
        acc = jnp.broadcast_to(b_ref[...], (rows, CONV_WIDTH))
        for k in range(CONV_KERNEL):
            base, s = divmod(k + off, SUBLANES)
            lo = r0 + base * SUBLANES
            tap = ext_ref[lo:lo + rows, :] if s == 0 else sh_ref[s - 1, lo:lo + rows, :]
            acc = acc + w[k:k + 1] * tap
        mu = jnp.mean(acc, axis=-1, keepdims=True)
        xc = acc - mu
        y = xc * lax.rsqrt(jnp.mean(xc * xc, axis=-1, keepdims=True) + NORM_EPS) * lg_ref[...] + lb_ref[...]
        z_ref[0, r0:r0 + rows, :] = (y * jax.nn.sigmoid(y)).astype(BF16)


def _inproj_kernel(x_ref, xp_ref, xn_ref, mod_ref, g1_ref, w1_ref, qg_ref, wq_ref, kvg_ref, wk_ref,
                   ct_ref, sa_ref, sb_ref, cw_ref, cb_ref, lg_ref, lb_ref,
                   z_ref, q_ref, k_ref, vt_ref, gc_ref, ga_ref, ext_ref, sh_ref, *, conv_rows):
    i = pl.program_id(1)
    tm = x_ref.shape[1]
    mod = mod_ref[0]
    norm_mod = lambda x: (_rms(x, g1_ref[...]) * (1.0 + mod[1:2]) + mod[0:1]).astype(BF16)
    h_ext = jnp.concatenate([norm_mod(xp_ref[0]), norm_mod(x_ref[0]), norm_mod(xn_ref[0])], axis=0)
    h = h_ext[CONV_HALO:CONV_HALO + tm]
    ct, sa, sb = ct_ref[...], sa_ref[...], sb_ref[...]

    conv_in = _dot(h_ext, w1_ref[:, COL_CONV:COL_QA])
    u = conv_in[:, :CONV_WIDTH] * jax.nn.sigmoid(conv_in[:, CONV_WIDTH:])
    row = lax.broadcasted_iota(jnp.int32, (tm + 2 * CONV_HALO, 1), 0)
    inside = jnp.logical_and(jnp.logical_or(row >= CONV_HALO, i > 0),
                             jnp.logical_or(row < CONV_HALO + tm, i < pl.num_programs(1) - 1))
    ext_ref[...] = jnp.where(inside, u, 0.0)
    _conv_branch(ext_ref, sh_ref, cw_ref, cb_ref, lg_ref, lb_ref, z_ref, rows=conv_rows)

    q_a = _dot(h, w1_ref[:, COL_QA:COL_KVA])
    q = _dot(_rms(q_a, qg_ref[...]).astype(BF16), wq_ref[...])
    for hd in range(N_HEADS):
        qh = _rope(q[:, hd * HEAD_DIM:(hd + 1) * HEAD_DIM], ct, sa, sb)
        q_ref[0, hd] = (qh * QK_SCALE_LOG2E).astype(BF16)

    kv_a = _dot(h, w1_ref[:, COL_KVA:COL_KR])
    kv_n = _rms(kv_a, kvg_ref[...]).astype(BF16)
    k_rope = _rope(_dot(h, w1_ref[:, COL_KR:COL_GATE]), ct, sa, sb)
    kv = _dot(kv_n, wk_ref[...])
    lane = lax.broadcasted_iota(jnp.int32, k_rope.shape, 1)
    ones = jnp.ones((HEAD_DIM - V_HEAD_DIM, tm), F32)
    for hd in range(N_HEADS):
        kv_h = kv[:, hd * HEAD_DIM:(hd + 1) * HEAD_DIM]
        k_ref[0, hd] = jnp.where(lane < QK_NOPE_DIM, kv_h, k_rope).astype(BF16)
        vt_ref[0, hd, 0] = jnp.concatenate([kv_h.T[QK_NOPE_DIM:], ones], axis=0).astype(BF16)

    gates = _dot(h, w1_ref[:, COL_GATE:COL_END])
    gc_ref[0] = gates[:, :D_MODEL].astype(BF16)
    ga_ref[0] = gates[:, D_MODEL:].astype(BF16)


def _inproj(x, mod, g1, w1, qg, wq, kvg, wk, ct, sa, sb, conv_w, conv_b, ln_g, ln_b, *, tm, tk, conv_rows=64):
    b, s, _ = x.shape
    per_chunk = tk // tm
    nh = tm // CONV_HALO
    last_halo = s // CONV_HALO - 1
    tok = lambda width: pl.BlockSpec((1, tm, width), lambda bi, i: (bi, i, 0))
    head = pl.BlockSpec((1, N_HEADS, tm, HEAD_DIM), lambda bi, i: (bi, 0, i, 0))
    head_t = pl.BlockSpec((1, N_HEADS, 1, HEAD_DIM, tm), lambda bi, i: (bi, 0, i // per_chunk, 0, i % per_chunk))
    tab = pl.BlockSpec((tm, LANES), lambda bi, i: (i, 0))
    return pl.pallas_call(
        functools.partial(_inproj_kernel, conv_rows=conv_rows),
        grid=(b, s // tm),
        in_specs=[
            tok(D_MODEL),
            pl.BlockSpec((1, CONV_HALO, D_MODEL), lambda bi, i: (bi, jnp.maximum(i * nh - 1, 0), 0)),
            pl.BlockSpec((1, CONV_HALO, D_MODEL), lambda bi, i: (bi, jnp.minimum((i + 1) * nh, last_halo), 0)),
            pl.BlockSpec((1, N_MOD, D_MODEL), lambda bi, i: (bi, 0, 0)),
            _const_spec((1, D_MODEL)),
            _const_spec((D_MODEL, COL_END)),
            _const_spec((1, Q_LORA_RANK)),
            _const_spec((Q_LORA_RANK, N_HEADS * HEAD_DIM)),
            _const_spec((1, KV_LORA_RANK)),
            _const_spec((KV_LORA_RANK, N_HEADS * HEAD_DIM)),
            tab, tab, tab,
            _const_spec((CONV_KERNEL, CONV_WIDTH)),
            _const_spec((1, CONV_WIDTH)),
            _const_spec((1, CONV_WIDTH)),
            _const_spec((1, CONV_WIDTH)),
        ],
        out_specs=[tok(CONV_WIDTH), head, head, head_t, tok(D_MODEL), tok(D_MODEL)],
        out_shape=[
            jax.ShapeDtypeStruct((b, s, CONV_WIDTH), BF16),
            jax.ShapeDtypeStruct((b, N_HEADS, s, HEAD_DIM), BF16),
            jax.ShapeDtypeStruct((b, N_HEADS, s, HEAD_DIM), BF16),
            jax.ShapeDtypeStruct((b, N_HEADS, s // tk, HEAD_DIM, tk), BF16),
            jax.ShapeDtypeStruct((b, s, D_MODEL), BF16),
            jax.ShapeDtypeStruct((b, s, D_MODEL), BF16),
        ],
        scratch_shapes=[
            pltpu.VMEM((tm + 2 * CONV_HALO, CONV_WIDTH), F32),
            pltpu.VMEM((SUBLANES - 1, tm + 2 * CONV_HALO - SUBLANES, CONV_WIDTH), F32),
        ],
        compiler_params=pltpu.CompilerParams(
            dimension_semantics=("arbitrary", "arbitrary"), vmem_limit_bytes=VMEM_LIMIT_BYTES),
        name="in_proj",
    )(x, x, x, mod, g1, w1, qg, wq, kvg, wk, ct, sa, sb, conv_w, conv_b, ln_g, ln_b)


def _attn_kernel(q_ref, k_ref, vt_ref, o_ref, s0_ref, s1_ref, x0_ref, x1_ref):
    tq = s0_ref.shape[2]
    n_chunks, _, tk = vt_ref.shape[2:]
    n_q = q_ref.shape[2] // tq
    max_of = {id(s0_ref): x0_ref, id(s1_ref): x1_ref}

    def scores(s_ref, hh, qi, c):
        kc = k_ref[0, hh, pl.ds(pl.multiple_of(c * tk, tk), tk), :]
        qt = q_ref[0, hh, pl.ds(pl.multiple_of(qi * tq, tq), tq), :]
        st = lax.dot_general(kc, qt, (((1,), (1,)), ((), ())), preferred_element_type=F32)
        s_ref[hh] = st
        max_of[id(s_ref)][hh] = jnp.max(st, axis=0, keepdims=True)

    def update(s_ref, hh, c, state):
        m, acc = state
        st = s_ref[hh]
        m_new = jnp.maximum(m, max_of[id(s_ref)][hh])
        alpha = jnp.exp2(m - m_new)
        pt = jnp.exp2(st - m_new).astype(BF16)
        return m_new, alpha * acc + _dot(vt_ref[0, hh, c], pt)

    def step(c, cur_ref, nxt_ref, state, nxt):
        new = []
        for hh in range(2):
            if nxt is not None:
                scores(nxt_ref, hh, *nxt)
            new.append(update(cur_ref, hh, c, state[hh]))
        return tuple(new)

    def pairs(qi, c0, state, n_pairs, after=False):
        for p in range(n_pairs):
            c = c0 + 2 * p
            ends = after is not False and p == n_pairs - 1
            state = step(c, s0_ref, s1_ref, state, (qi, c + 1))
            state = step(c + 1, s1_ref, s0_ref, state, after if ends else (qi, c + 2))
        return state

    per_iter = 2 if n_chunks % 4 == 0 else 1
    n_iter = n_chunks // (2 * per_iter) - 1

    def q_tile(qi, is_last):
        state = tuple((jnp.full((1, tq), -jnp.inf, F32), jnp.zeros((HEAD_DIM, tq), F32)) for _ in range(2))
        state = lax.fori_loop(0, n_iter, lambda j, st: pairs(qi, j * (2 * per_iter), st, per_iter), state)
        state = pairs(qi, n_iter * 2 * per_iter, state, per_iter, after=None if is_last else (qi + 1, 0))
        ot = jnp.concatenate([acc[:V_HEAD_DIM] / acc[V_HEAD_DIM:V_HEAD_DIM + 1] for _, acc in state], axis=0)
        o_ref[0, pl.ds(pl.multiple_of(qi * tq, tq), tq), :] = ot.T.astype(BF16)

    for hh in range(2):
        scores(s0_ref, hh, 0, 0)

    unroll = 2 if n_chunks <= 2 else 1

    def body(j, carry):
        for r in range(unroll):
            q_tile(j * unroll + r, False)
        return carry

    lax.fori_loop(0, n_q // unroll - 1, body, 0)
    for r in range(unroll):
        q_tile(n_q - unroll + r, r == unroll - 1)


def _attention(q, k, vt, *, tq):
    b, _, s, _ = q.shape
    n_chunks, _, tk = vt.shape[2:]
    return pl.pallas_call(
        _attn_kernel,
        grid=(b, N_HEADS // 2),
        in_specs=[
            pl.BlockSpec((1, 2, s, HEAD_DIM), lambda bi, hp: (bi, hp, 0, 0)),
            pl.BlockSpec((1, 2, s, HEAD_DIM), lambda bi, hp: (bi, hp, 0, 0)),
            pl.BlockSpec((1, 2, n_chunks, HEAD_DIM, tk), lambda bi, hp: (bi, hp, 0, 0, 0)),
        ],
        out_specs=pl.BlockSpec((1, s, LANES), lambda bi, hp: (bi, 0, hp)),
        out_shape=jax.ShapeDtypeStruct((b, s, N_HEADS * V_HEAD_DIM), BF16),
        scratch_shapes=[pltpu.VMEM((2, tk, tq), F32), pltpu.VMEM((2, tk, tq), F32),
                        pltpu.VMEM((2, 1, tq), F32), pltpu.VMEM((2, 1, tq), F32)],
        compiler_params=pltpu.CompilerParams(
            dimension_semantics=("arbitrary", "arbitrary"), vmem_limit_bytes=VMEM_LIMIT_BYTES),
        name="attention",
    )(q, k, vt)


def _outproj_kernel(x_ref, mod_ref, o_ref, z_ref, gc_ref, ga_ref, wao_ref, wco_ref, wout_ref, g2_ref,
                    wup_ref, wdn_ref, fg_ref, y_ref, *, final):
    x = x_ref[0]
    mod = mod_ref[0]
    y_attn = _dot(o_ref[0], wao_ref[...])
    y_conv = _dot(z_ref[0], wco_ref[...])
    g_conv = jax.nn.sigmoid(gc_ref[0].astype(F32))
    g_attn = jax.nn.sigmoid(ga_ref[0].astype(F32))
    mix_in = (g_conv * y_conv + g_attn * y_attn).astype(BF16)
    x1 = x + mod[2:3] * _dot(mix_in, wout_ref[...])
    h2 = (_rms(x1, g2_ref[...]) * (1.0 + mod[4:5]) + mod[3:4]).astype(BF16)
    acc = jnp.zeros(x.shape, F32)
    for c in range(0, D_FF, FF_CHUNK):
        up = _dot(h2, wup_ref[:, c:c + FF_CHUNK])
        acc = acc + _dot(jnp.square(jnp.maximum(up, 0.0)).astype(BF16), wdn_ref[c:c + FF_CHUNK, :])
    x2 = x1 + mod[5:6] * acc
    if final:
        x2 = _rms(x2, fg_ref[...])
    y_ref[0] = x2


def _outproj(x, mod, o, z, gc, ga, wao, wco, wout, g2, wup, wdn, fg, *, tm, final):
    b, s, _ = x.shape
    tok = lambda width: pl.BlockSpec((1, tm, width), lambda bi, i: (bi, i, 0))
    return pl.pallas_call(
        functools.partial(_outproj_kernel, final=final),
        grid=(b, s // tm),
        in_specs=[
            tok(D_MODEL),
            pl.BlockSpec((1, N_MOD, D_MODEL), lambda bi, i: (bi, 0, 0)),
            tok(N_HEADS * V_HEAD_DIM), tok(CONV_WIDTH), tok(D_MODEL), tok(D_MODEL),
            _const_spec((N_HEADS * V_HEAD_DIM, D_MODEL)),
            _const_spec((CONV_WIDTH, D_MODEL)),
            _const_spec((D_MODEL, D_MODEL)),
            _const_spec((1, D_MODEL)),
            _const_spec((D_MODEL, D_FF)),
            _const_spec((D_FF, D_MODEL)),
            _const_spec((1, D_MODEL)),
        ],
        out_specs=tok(D_MODEL),
        out_shape=jax.ShapeDtypeStruct((b, s, D_MODEL), F32),
        compiler_params=pltpu.CompilerParams(
            dimension_semantics=("arbitrary", "arbitrary"), vmem_limit_bytes=VMEM_LIMIT_BYTES),
        name="out_proj_mlp",
    )(x, mod, o, z, gc, ga, wao, wco, wout, g2, wup, wdn, fg)


def _rope_tables(seq_len):
    inv = ROPE_BASE ** (-jnp.arange(0, QK_ROPE_DIM, 2, dtype=F32) / QK_ROPE_DIM)
    ang = jnp.arange(seq_len, dtype=F32)[:, None] * inv[None, :]
    cos, sin = jnp.cos(ang), jnp.sin(ang)
    one = jnp.ones((seq_len, QK_NOPE_DIM), F32)
    zero = jnp.zeros((seq_len, QK_NOPE_DIM), F32)
    half = jnp.zeros_like(sin)
    ct = jnp.concatenate([one, cos, cos], axis=-1)
    sa = jnp.concatenate([zero, -sin, half], axis=-1)
    sb = jnp.concatenate([zero, half, sin], axis=-1)
    return ct, sa, sb


def _pack_w_in(w_in):
    c0 = 2 * CONV_WIDTH
    c1 = c0 + Q_LORA_RANK
    c2 = c1 + KV_LORA_RANK
    c3 = c2 + QK_ROPE_DIM
    pad = jnp.zeros((w_in.shape[0], LANES - QK_ROPE_DIM), w_in.dtype)
    return jnp.concatenate([w_in[:, :c2], pad, w_in[:, c2:c3], w_in[:, c3:]], axis=-1).astype(BF16)


def _trunk(x, mods, layer_weights, final_g, *, tm, tq, tk):
    ct, sa, sb = _rope_tables(x.shape[1])
    n_layers = len(layer_weights)
    for l, w in enumerate(layer_weights):
        mod = mods[l]
        z, q, k, vt, gc, ga = _inproj(x, mod, w["g1"], w["w1"], w["qg"], w["wq"], w["kvg"], w["wk"], ct, sa, sb,
                                      w["conv_w"], w["conv_b"], w["ln_g"], w["ln_b"], tm=tm, tk=tk)
        o = _attention(q, k, vt, tq=tq)
        x = _outproj(x, mod, o, z, gc, ga, w["wao"], w["wco"], w["wout"], w["g2"], w["wup"], w["wdn"],
                     final_g, tm=tm, final=(l == n_layers - 1))
    return x


def kernel(x_prompt, x_sample, c_prompt, c_sample, ada_w, ada_b, norm_mix_g, w_in, q_norm_g, w_q_up, kv_norm_g, w_kv_up, w_attn_o, conv_dw, conv_dw_b, conv_ln_g, conv_ln_b, w_conv_out, w_out, norm_mlp_g, w_mlp_up, w_mlp_down, final_g):
    n_layers = ada_w.shape[0]
    bp, bs = c_prompt.shape[0], c_sample.shape[0]
    pad_rows = -(bp + bs) % 8
    c_all = jnp.concatenate([c_prompt, c_sample, jnp.zeros((pad_rows, D_MODEL), F32)], axis=0)
    mods = _ada(c_all, ada_w, ada_b)
    mods_p = mods[:, :bp].reshape(n_layers, bp, N_MOD, D_MODEL)
    mods_s = mods[:, bp:bp + bs].reshape(n_layers, bs, N_MOD, D_MODEL)

    row = lambda a: a.reshape(1, -1)
    layer_weights = []
    for l in range(n_layers):
        layer_weights.append(dict(
            g1=row(norm_mix_g[l]), w1=_pack_w_in(w_in[l]), qg=row(q_norm_g[l]), wq=w_q_up[l].astype(BF16),
            kvg=row(kv_norm_g[l]), wk=w_kv_up[l].astype(BF16),
            conv_w=conv_dw[l], conv_b=row(conv_dw_b[l]), ln_g=row(conv_ln_g[l]), ln_b=row(conv_ln_b[l]),
            wao=w_attn_o[l].astype(BF16), wco=w_conv_out[l].astype(BF16), wout=w_out[l].astype(BF16),
            g2=row(norm_mlp_g[l]), wup=w_mlp_up[l].astype(BF16), wdn=w_mlp_down[l].astype(BF16)))
    fg = row(final_g)
    tiles = dict(tm=512, tk=1024, tq=256)
    y_prompt = _trunk(x_prompt, mods_p, layer_weights, fg, **tiles)
    y_sample = _trunk(x_sample, mods_s, layer_weights, fg, **tiles)
    return (y_prompt, y_sample)
```

```python
import functools

import jax
import jax.numpy as jnp
from jax import lax
from jax.experimental import pallas as pl
from jax.experimental.pallas import tpu as pltpu

D_MODEL = 1024
N_HEADS = 8
QK_NOPE_DIM = 64
QK_ROPE_DIM = 64
V_HEAD_DIM = 64
HEAD_DIM = QK_NOPE_DIM + QK_ROPE_DIM
Q_LORA_RANK = 384
KV_LORA_RANK = 256
CONV_WIDTH = 512
CONV_KERNEL = 31
CONV_HALO = 16
D_FF = 4 * D_MODEL
FF_CHUNK = 1024
ATTN_ROWS_PER_STEP = 16384
NORM_EPS = 1e-6
QK_SCALE_LOG2E = (QK_NOPE_DIM + QK_ROPE_DIM) ** -0.5 * 1.4426950408889634
N_MOD = 6
ROPE_BASE = 10000.0
LANES = 128
SUBLANES = 8
VMEM_LIMIT_BYTES = 56 * 1024 * 1024

COL_CONV = 0
COL_QA = COL_CONV + 2 * CONV_WIDTH
COL_KVA = COL_QA + Q_LORA_RANK
COL_KR = COL_KVA + KV_LORA_RANK
COL_GATE = COL_KR + LANES
COL_END = COL_GATE + 2 * D_MODEL

F32 = jnp.float32
BF16 = jnp.bfloat16


def _rms(x, g):
    return x * lax.rsqrt(jnp.mean(x * x, axis=-1, keepdims=True) + NORM_EPS) * g


def _dot(a, b):
    return jnp.dot(a, b, preferred_element_type=F32)


def _const_spec(shape):
    return pl.BlockSpec(shape, lambda *_: (0,) * len(shape), pipeline_mode=pl.Buffered(1))


def _rope(t, ct, sa, sb):
    return t * ct + pltpu.roll(t, LANES - QK_ROPE_DIM // 2, 1) * sa + pltpu.roll(t, QK_ROPE_DIM // 2, 1) * sb


def _ada_kernel(c_ref, w_ref, b_ref, o_ref):
    c = c_ref[...]
    sc = (c * jax.nn.sigmoid(c)).astype(BF16)
    o_ref[0] = _dot(sc, w_ref[0].astype(BF16)) + b_ref[0]


def _ada(c_all, ada_w, ada_b):
    n_layers = ada_w.shape[0]
    bp = c_all.shape[0]
    return pl.pallas_call(
        _ada_kernel,
        grid=(n_layers, N_MOD),
        in_specs=[
            pl.BlockSpec((bp, D_MODEL), lambda l, j: (0, 0)),
            pl.BlockSpec((1, D_MODEL, D_MODEL), lambda l, j: (l, 0, j)),
            pl.BlockSpec((1, 1, D_MODEL), lambda l, j: (l, 0, j)),
        ],
        out_specs=pl.BlockSpec((1, bp, D_MODEL), lambda l, j: (l, 0, j)),
        out_shape=jax.ShapeDtypeStruct((n_layers, bp, N_MOD * D_MODEL), F32),
        compiler_params=pltpu.CompilerParams(dimension_semantics=("arbitrary", "arbitrary")),
        name="ada_mod",
    )(c_all, ada_w, ada_b.reshape(n_layers, 1, N_MOD * D_MODEL))


def _conv_branch(ext_ref, sh_ref, w_ref, b_ref, lg_ref, lb_ref, z_ref, *, rows):
    tm = z_ref.shape[1]
    n_sh = sh_ref.shape[1]
    for s in range(1, SUBLANES):
        sh_ref[s - 1] = ext_ref[s:s + n_sh, :]
    w = w_ref[...]
    off = CONV_HALO - CONV_KERNEL // 2
    for r0 in range(0, tm, rows):
        acc = jnp.broadcast_to(b_ref[...], (rows, CONV_WIDTH))
        for k in range(CONV_KERNEL):
            base, s = divmod(k + off, SUBLANES)
            lo = r0 + base * SUBLANES
            tap = ext_ref[lo:lo + rows, :] if s == 0 else sh_ref[s - 1, lo:lo + rows, :]
            acc = acc + w[k:k + 1] * tap
        mu = jnp.mean(acc, axis=-1, keepdims=True)
        xc = acc - mu
        y = xc * lax.rsqrt(jnp.mean(xc * xc, axis=-1, keepdims=True) + NORM_EPS) * lg_ref[...] + lb_ref[...]
        z_ref[0, r0:r0 + rows, :] = (y * jax.nn.sigmoid(y)).astype(BF16)


def _inproj_kernel(x_ref, xp_ref, xn_ref, mod_ref, g1_ref, w1_ref, qg_ref, wq_ref, kvg_ref, wk_ref,
                   ct_ref, sa_ref, sb_ref, cw_ref, cb_ref, lg_ref, lb_ref,
                   z_ref, q_ref, k_ref, vt_ref, gc_ref, ga_ref, ext_ref, sh_ref, *, conv_rows):
    i = pl.program_id(1)
    tm = x_ref.shape[1]
    mod = mod_ref[0]
    norm_mod = lambda x: (_rms(x, g1_ref[...]) * (1.0 + mod[1:2]) + mod[0:1]).astype(BF16)
    h_ext = jnp.concatenate([norm_mod(xp_ref[0]), norm_mod(x_ref[0]), norm_mod(xn_ref[0])], axis=0)
    h = h_ext[CONV_HALO:CONV_HALO + tm]
    ct, sa, sb = ct_ref[...], sa_ref[...], sb_ref[...]

    conv_in = _dot(h_ext, w1_ref[:, COL_CONV:COL_QA])
    u = conv_in[:, :CONV_WIDTH] * jax.nn.sigmoid(conv_in[:, CONV_WIDTH:])
    row = lax.broadcasted_iota(jnp.int32, (tm + 2 * CONV_HALO, 1), 0)
    inside = jnp.logical_and(jnp.logical_or(row >= CONV_HALO, i > 0),
                             jnp.logical_or(row < CONV_HALO + tm, i < pl.num_programs(1) - 1))
    ext_ref[...] = jnp.where(inside, u, 0.0)
    _conv_branch(ext_ref, sh_ref, cw_ref, cb_ref, lg_ref, lb_ref, z_ref, rows=conv_rows)

    q_a = _dot(h, w1_ref[:, COL_QA:COL_KVA])
    q = _dot(_rms(q_a, qg_ref[...]).astype(BF16), wq_ref[...])
    for hd in range(N_HEADS):
        qh = _rope(q[:, hd * HEAD_DIM:(hd + 1) * HEAD_DIM], ct, sa, sb)
        q_ref[0, hd] = (qh * QK_SCALE_LOG2E).astype(BF16)

    kv_a = _dot(h, w1_ref[:, COL_KVA:COL_KR])
    kv_n = _rms(kv_a, kvg_ref[...]).astype(BF16)
    k_rope = _rope(_dot(h, w1_ref[:, COL_KR:COL_GATE]), ct, sa, sb)
    kv = _dot(kv_n, wk_ref[...])
    lane = lax.broadcasted_iota(jnp.int32, k_rope.shape, 1)
    ones = jnp.ones((HEAD_DIM - V_HEAD_DIM, tm), F32)
    for hd in range(N_HEADS):
        kv_h = kv[:, hd * HEAD_DIM:(hd + 1) * HEAD_DIM]
        k_ref[0, hd] = jnp.where(lane < QK_NOPE_DIM, kv_h, k_rope).astype(BF16)
        vt_ref[0, hd, 0] = jnp.concatenate([kv_h.T[QK_NOPE_DIM:], ones], axis=0).astype(BF16)

    gates = _dot(h, w1_ref[:, COL_GATE:COL_END])
    gc_ref[0] = gates[:, :D_MODEL].astype(BF16)
    ga_ref[0] = gates[:, D_MODEL:].astype(BF16)


def _inproj(x, mod, g1, w1, qg, wq, kvg, wk, ct, sa, sb, conv_w, conv_b, ln_g, ln_b, *, tm, tk, conv_rows=32):
    b, s, _ = x.shape
    per_chunk = tk // tm
    nh = tm // CONV_HALO
    last_halo = s // CONV_HALO - 1
    tok = lambda width: pl.BlockSpec((1, tm, width), lambda bi, i: (bi, i, 0))
    head = pl.BlockSpec((1, N_HEADS, tm, HEAD_DIM), lambda bi, i: (bi, 0, i, 0))
    head_t = pl.BlockSpec((1, N_HEADS, 1, HEAD_DIM, tm), lambda bi, i: (bi, 0, i // per_chunk, 0, i % per_chunk))
    tab = pl.BlockSpec((tm, LANES), lambda bi, i: (i, 0))
    return pl.pallas_call(
        functools.partial(_inproj_kernel, conv_rows=conv_rows),
        grid=(b, s // tm),
        in_specs=[
            tok(D_MODEL),
            pl.BlockSpec((1, CONV_HALO, D_MODEL), lambda bi, i: (bi, jnp.maximum(i * nh - 1, 0), 0)),
            pl.BlockSpec((1, CONV_HALO, D_MODEL), lambda bi, i: (bi, jnp.minimum((i + 1) * nh, last_halo), 0)),
            pl.BlockSpec((1, N_MOD, D_MODEL), lambda bi, i: (bi, 0, 0)),
            _const_spec((1, D_MODEL)),
            _const_spec((D_MODEL, COL_END)),
            _const_spec((1, Q_LORA_RANK)),
            _const_spec((Q_LORA_RANK, N_HEADS * HEAD_DIM)),
            _const_spec((1, KV_LORA_RANK)),
            _const_spec((KV_LORA_RANK, N_HEADS * HEAD_DIM)),
            tab, tab, tab,
            _const_spec((CONV_KERNEL, CONV_WIDTH)),
            _const_spec((1, CONV_WIDTH)),
            _const_spec((1, CONV_WIDTH)),
            _const_spec((1, CONV_WIDTH)),
        ],
        out_specs=[tok(CONV_WIDTH), head, head, head_t, tok(D_MODEL), tok(D_MODEL)],
        out_shape=[
            jax.ShapeDtypeStruct((b, s, CONV_WIDTH), BF16),
            jax.ShapeDtypeStruct((b, N_HEADS, s, HEAD_DIM), BF16),
            jax.ShapeDtypeStruct((b, N_HEADS, s, HEAD_DIM), BF16),
            jax.ShapeDtypeStruct((b, N_HEADS, s // tk, HEAD_DIM, tk), BF16),
            jax.ShapeDtypeStruct((b, s, D_MODEL), BF16),
            jax.ShapeDtypeStruct((b, s, D_MODEL), BF16),
        ],
        scratch_shapes=[
            pltpu.VMEM((tm + 2 * CONV_HALO, CONV_WIDTH), F32),
            pltpu.VMEM((SUBLANES - 1, tm + 2 * CONV_HALO - SUBLANES, CONV_WIDTH), F32),
        ],
        compiler_params=pltpu.CompilerParams(
            dimension_semantics=("arbitrary", "arbitrary"), vmem_limit_bytes=VMEM_LIMIT_BYTES),
        name="in_proj",
    )(x, x, x, mod, g1, w1, qg, wq, kvg, wk, ct, sa, sb, conv_w, conv_b, ln_g, ln_b)


def _attn_kernel(q_ref, k_ref, vt_ref, o_ref, s0_ref, s1_ref, x0_ref, x1_ref):
    tq = s0_ref.shape[2]
    n_chunks, _, tk = vt_ref.shape[2:]
    n_q = q_ref.shape[2] // tq
    hps = q_ref.shape[1]
    max_of ={id(s0_ref): x0_ref, id(s1_ref): x1_ref}

    def scores(s_ref, hh, qi, c):
        kc = k_ref[0, hh, pl.ds(pl.multiple_of(c * tk, tk), tk), :]
        qt = q_ref[0, hh, pl.ds(pl.multiple_of(qi * tq, tq), tq), :]
        st = lax.dot_general(kc, qt, (((1,), (1,)), ((), ())), preferred_element_type=F32)
        s_ref[hh] = st
        max_of[id(s_ref)][hh] = jnp.max(st, axis=0, keepdims=True)

    def update(s_ref, hh, c, state):
        m, acc = state
        st = s_ref[hh]
        m_new = jnp.maximum(m, max_of[id(s_ref)][hh])
        alpha = jnp.exp2(m - m_new)
        pt = jnp.exp2(st - m_new).astype(BF16)
        return m_new, alpha * acc + _dot(vt_ref[0, hh, c], pt)

    def step(c, cur_ref, nxt_ref, state, nxt):
        new = []
        for hh in range(hps):
            if nxt is not None:
                scores(nxt_ref, hh, *nxt)
            new.append(update(cur_ref, hh, c, state[hh]))
        return tuple(new)

    def pairs(qi, c0, state, n_pairs, after=False):
        for p in range(n_pairs):
            c = c0 + 2 * p
            ends = after is not False and p == n_pairs - 1
            state = step(c, s0_ref, s1_ref, state, (qi, c + 1))
            state = step(c + 1, s1_ref, s0_ref, state, after if ends else (qi, c + 2))
        return state

    per_iter = 2 if n_chunks % 4 == 0 else 1
    n_iter = n_chunks // (2 * per_iter) - 1

    def q_tile(qi, is_last):
        state = tuple((jnp.full((1, tq), -jnp.inf, F32), jnp.zeros((HEAD_DIM, tq), F32)) for _ in range(hps))
        state = lax.fori_loop(0, n_iter, lambda j, st: pairs(qi, j * (2 * per_iter), st, per_iter), state)
        state = pairs(qi, n_iter * 2 * per_iter, state, per_iter, after=None if is_last else (qi + 1, 0))
        ot = jnp.concatenate([acc[:V_HEAD_DIM] / acc[V_HEAD_DIM:V_HEAD_DIM + 1] for _, acc in state], axis=0)
        o_ref[0, pl.ds(pl.multiple_of(qi * tq, tq), tq), :] = ot.T.astype(BF16)

    for hh in range(hps):
        scores(s0_ref, hh, 0, 0)

    unroll = 2 if n_chunks * hps <= 8 else 1

    def body(j, carry):
        for r in range(unroll):
            q_tile(j * unroll + r, False)
        return carry

    lax.fori_loop(0, n_q // unroll - 1, body, 0)
    for r in range(unroll):
        q_tile(n_q - unroll + r, r == unroll - 1)


def _attention(q, k, vt, *, tq):
    b, _, s, _ = q.shape
    n_chunks, _, tk = vt.shape[2:]
    hps = min(4, ATTN_ROWS_PER_STEP // s)
    return pl.pallas_call(
        _attn_kernel,
        grid=(b, N_HEADS // hps),
        in_specs=[
            pl.BlockSpec((1, hps, s, HEAD_DIM), lambda bi, hp: (bi, hp, 0, 0)),
            pl.BlockSpec((1, hps, s, HEAD_DIM), lambda bi, hp: (bi, hp, 0, 0)),
            pl.BlockSpec((1, hps, n_chunks, HEAD_DIM, tk), lambda bi, hp: (bi, hp, 0, 0, 0)),
        ],
        out_specs=pl.BlockSpec((1, s, hps * V_HEAD_DIM), lambda bi, hp: (bi, 0, hp)),
        out_shape=jax.ShapeDtypeStruct((b, s, N_HEADS * V_HEAD_DIM), BF16),
        scratch_shapes=[pltpu.VMEM((hps, tk, tq), F32), pltpu.VMEM((hps, tk, tq), F32),
                        pltpu.VMEM((hps, 1, tq), F32), pltpu.VMEM((hps, 1, tq), F32)],
        compiler_params=pltpu.CompilerParams(
            dimension_semantics=("arbitrary", "arbitrary"), vmem_limit_bytes=VMEM_LIMIT_BYTES),
        name="attention",
    )(q, k, vt)


def _outproj_kernel(x_ref, mod_ref, o_ref, z_ref, gc_ref, ga_ref, wao_ref, wco_ref, wout_ref, g2_ref,
                    wup_ref, wdn_ref, fg_ref, y_ref, *, final):
    x = x_ref[0]
    mod = mod_ref[0]
    y_attn = _dot(o_ref[0], wao_ref[...])
    y_conv = _dot(z_ref[0], wco_ref[...])
    g_conv = jax.nn.sigmoid(gc_ref[0].astype(F32))
    g_attn = jax.nn.sigmoid(ga_ref[0].astype(F32))
    mix_in = (g_conv * y_conv + g_attn * y_attn).astype(BF16)
    x1 = x + mod[2:3] * _dot(mix_in, wout_ref[...])
    h2 = (_rms(x1, g2_ref[...]) * (1.0 + mod[4:5]) + mod[3:4]).astype(BF16)
    acc = jnp.zeros(x.shape, F32)
    for c in range(0, D_FF, FF_CHUNK):
        up = _dot(h2, wup_ref[:, c:c + FF_CHUNK])
        acc = acc + _dot(jnp.square(jnp.maximum(up, 0.0)).astype(BF16), wdn_ref[c:c + FF_CHUNK, :])
    x2 = x1 + mod[5:6] * acc
    if final:
        x2 = _rms(x2, fg_ref[...])
    y_ref[0] = x2


def _outproj(x, mod, o, z, gc, ga, wao, wco, wout, g2, wup, wdn, fg, *, tm, final):
    b, s, _ = x.shape
    tok = lambda width: pl.BlockSpec((1, tm, width), lambda bi, i: (bi, i, 0))
    return pl.pallas_call(
        functools.partial(_outproj_kernel, final=final),
        grid=(b, s // tm),
        in_specs=[
            tok(D_MODEL),
            pl.BlockSpec((1, N_MOD, D_MODEL), lambda bi, i: (bi, 0, 0)),
            tok(N_HEADS * V_HEAD_DIM), tok(CONV_WIDTH), tok(D_MODEL), tok(D_MODEL),
            _const_spec((N_HEADS * V_HEAD_DIM, D_MODEL)),
            _const_spec((CONV_WIDTH, D_MODEL)),
            _const_spec((D_MODEL, D_MODEL)),
            _const_spec((1, D_MODEL)),
            _const_spec((D_MODEL, D_FF)),
            _const_spec((D_FF, D_MODEL)),
            _const_spec((1, D_MODEL)),
        ],
        out_specs=tok(D_MODEL),
        out_shape=jax.ShapeDtypeStruct((b, s, D_MODEL), F32),
        compiler_params=pltpu.CompilerParams(
            dimension_semantics=("arbitrary", "arbitrary"), vmem_limit_bytes=VMEM_LIMIT_BYTES),
        name="out_proj_mlp",
    )(x, mod, o, z, gc, ga, wao, wco, wout, g2, wup, wdn, fg)


def _rope_tables(seq_len):
    inv = ROPE_BASE ** (-jnp.arange(0, QK_ROPE_DIM, 2, dtype=F32) / QK_ROPE_DIM)
    ang = jnp.arange(seq_len, dtype=F32)[:, None] * inv[None, :]
    cos, sin = jnp.cos(ang), jnp.sin(ang)
    one = jnp.ones((seq_len, QK_NOPE_DIM), F32)
    zero = jnp.zeros((seq_len, QK_NOPE_DIM), F32)
    half = jnp.zeros_like(sin)
    ct = jnp.concatenate([one, cos, cos], axis=-1)
    sa = jnp.concatenate([zero, -sin, half], axis=-1)
    sb = jnp.concatenate([zero, half, sin], axis=-1)
    return ct, sa, sb


def _pack_w_in(w_in):
    c0 = 2 * CONV_WIDTH
    c1 = c0 + Q_LORA_RANK
    c2 = c1 + KV_LORA_RANK
    c3 = c2 + QK_ROPE_DIM
    pad = jnp.zeros((w_in.shape[0], LANES - QK_ROPE_DIM), w_in.dtype)
    return jnp.concatenate([w_in[:, :c2], pad, w_in[:, c2:c3], w_in[:, c3:]], axis=-1).astype(BF16)


def _trunk(x, mods, layer_weights, final_g, *, tm, tq, tk):
    ct, sa, sb = _rope_tables(x.shape[1])
    n_layers = len(layer_weights)
    for l, w in enumerate(layer_weights):
        mod = mods[l]
        z, q, k, vt, gc, ga = _inproj(x, mod, w["g1"], w["w1"], w["qg"], w["wq"], w["kvg"], w["wk"], ct, sa, sb,
                                      w["conv_w"], w["conv_b"], w["ln_g"], w["ln_b"], tm=tm, tk=tk)
        o = _attention(q, k, vt, tq=tq)
        x = _outproj(x, mod, o, z, gc, ga, w["wao"], w["wco"], w["wout"], w["g2"], w["wup"], w["wdn"],
                     final_g, tm=tm, final=(l == n_layers - 1))
    return x


def kernel(x_prompt, x_sample, c_prompt, c_sample, ada_w, ada_b, norm_mix_g, w_in, q_norm_g, w_q_up, kv_norm_g, w_kv_up, w_attn_o, conv_dw, conv_dw_b, conv_ln_g, conv_ln_b, w_conv_out, w_out, norm_mlp_g, w_mlp_up, w_mlp_down, final_g):
    n_layers = ada_w.shape[0]
    bp, bs = c_prompt.shape[0], c_sample.shape[0]
    pad_rows = -(bp + bs) % 8
    c_all = jnp.concatenate([c_prompt, c_sample, jnp.zeros((pad_rows, D_MODEL), F32)], axis=0)
    mods = _ada(c_all, ada_w, ada_b)
    mods_p = mods[:, :bp].reshape(n_layers, bp, N_MOD, D_MODEL)
    mods_s = mods[:, bp:bp + bs].reshape(n_layers, bs, N_MOD, D_MODEL)

    row = lambda a: a.reshape(1, -1)
    layer_weights = []
    for l in range(n_layers):
        layer_weights.append(dict(
            g1=row(norm_mix_g[l]), w1=_pack_w_in(w_in[l]), qg=row(q_norm_g[l]), wq=w_q_up[l].astype(BF16),
            kvg=row(kv_norm_g[l]), wk=w_kv_up[l].astype(BF16),
            conv_w=conv_dw[l], conv_b=row(conv_dw_b[l]), ln_g=row(conv_ln_g[l]), ln_b=row(conv_ln_b[l]),
            wao=w_attn_o[l].astype(BF16), wco=w_conv_out[l].astype(BF16), wout=w_out[l].astype(BF16),
            g2=row(norm_mlp_g[l]), wup=w_mlp_up[l].astype(BF16), wdn=w_mlp_down[l].astype(BF16)))
    fg = row(final_g)
    tiles = dict(tm=512, tk=1024, tq=256)
    y_prompt = _trunk(x_prompt, mods_p, layer_weights, fg, **tiles)
    y_sample = _trunk(x_sample, mods_s, layer_weights, fg, **tiles)
    return (y_prompt, y_sample)
```

```python
import functools

import jax
import jax.numpy as jnp
from jax import lax
from jax.experimental import pallas as pl
from jax.experimental.pallas import tpu as pltpu

D_MODEL = 1024
N_HEADS = 8
QK_NOPE_DIM = 64
QK_ROPE_DIM = 64
V_HEAD_DIM = 64
HEAD_DIM = QK_NOPE_DIM + QK_ROPE_DIM
Q_LORA_RANK = 384
KV_LORA_RANK = 256
CONV_WIDTH = 512
CONV_KERNEL = 31
CONV_HALO = 16
D_FF = 4 * D_MODEL
FF_CHUNK = 1024
ATTN_ROWS_PER_STEP = 16384
NORM_EPS = 1e-6
QK_SCALE_LOG2E = (QK_NOPE_DIM + QK_ROPE_DIM) ** -0.5 * 1.4426950408889634
N_MOD = 6
ROPE_BASE = 10000.0
LANES = 128
SUBLANES = 8
VMEM_LIMIT_BYTES = 56 * 1024 * 1024

COL_CONV = 0
COL_QA = COL_CONV + 2 * CONV_WIDTH
COL_KVA = COL_QA + Q_LORA_RANK
COL_KR = COL_KVA + KV_LORA_RANK
COL_GATE = COL_KR + LANES
COL_END = COL_GATE + 2 * D_MODEL

F32 = jnp.float32
BF16 = jnp.bfloat16


def _rms(x, g):
    return x * lax.rsqrt(jnp.mean(x * x, axis=-1, keepdims=True) + NORM_EPS) * g


def _dot(a, b):
    return jnp.dot(a, b, preferred_element_type=F32)


def _const_spec(shape):
    return pl.BlockSpec(shape, lambda *_: (0,) * len(shape), pipeline_mode=pl.Buffered(1))


def _rope(t, ct, sa, sb):
    return t * ct + pltpu.roll(t, LANES - QK_ROPE_DIM // 2, 1) * sa + pltpu.roll(t, QK_ROPE_DIM // 2, 1) * sb


def _ada_kernel(c_ref, w_ref, b_ref, o_ref):
    c = c_ref[...]
    sc = (c * jax.nn.sigmoid(c)).astype(BF16)
    o_ref[0] = _dot(sc, w_ref[0].astype(BF16)) + b_ref[0]


def _ada(c_all, ada_w, ada_b):
    n_layers = ada_w.shape[0]
    bp = c_all.shape[0]
    return pl.pallas_call(
        _ada_kernel,
        grid=(n_layers, N_MOD),
        in_specs=[
            pl.BlockSpec((bp, D_MODEL), lambda l, j: (0, 0)),
            pl.BlockSpec((1, D_MODEL, D_MODEL), lambda l, j: (l, 0, j)),
            pl.BlockSpec((1, 1, D_MODEL), lambda l, j: (l, 0, j)),
        ],
        out_specs=pl.BlockSpec((1, bp, D_MODEL), lambda l, j: (l, 0, j)),
        out_shape=jax.ShapeDtypeStruct((n_layers, bp, N_MOD * D_MODEL), F32),
        compiler_params=pltpu.CompilerParams(dimension_semantics=("arbitrary", "arbitrary")),
        name="ada_mod",
    )(c_all, ada_w, ada_b.reshape(n_layers, 1, N_MOD * D_MODEL))


def _conv_branch(ext_ref, sh_ref, w_ref, b_ref, lg_ref, lb_ref, z_ref, *, rows):
    tm = z_ref.shape[1]
    n_sh = sh_ref.shape[1]
    for s in range(1, SUBLANES):
        sh_ref[s - 1] = ext_ref[s:s + n_sh, :]
    w = w_ref[...]
    off = CONV_HALO - CONV_KERNEL // 2
    for r0 in range(0, tm, rows):
        acc = jnp.broadcast_to(b_ref[...], (rows, CONV_WIDTH))
        for k in range(CONV_KERNEL):
            base, s = divmod(k + off, SUBLANES)
            lo = r0 + base * SUBLANES
            tap = ext_ref[lo:lo + rows, :] if s == 0 else sh_ref[s - 1, lo:lo + rows, :]
            acc = acc + w[k:k + 1] * tap
        mu = jnp.mean(acc, axis=-1, keepdims=True)
        xc = acc - mu
        y = xc * lax.rsqrt(jnp.mean(xc * xc, axis=-1, keepdims=True) + NORM_EPS) * lg_ref[...] + lb_ref[...]
        z_ref[0, r0:r0 + rows, :] = (y * jax.nn.sigmoid(y)).astype(BF16)


def _inproj_kernel(x_ref, xp_ref, xn_ref, mod_ref, g1_ref, w1_ref, qg_ref, wq_ref, kvg_ref, wk_ref,
                   ct_ref, sa_ref, sb_ref, cw_ref, cb_ref, lg_ref, lb_ref,
                   z_ref, q_ref, k_ref, vt_ref, gc_ref, ga_ref, ext_ref, sh_ref, *, conv_rows):
    i = pl.program_id(1)
    tm = x_ref.shape[1]
    mod = mod_ref[0]
    norm_mod = lambda x: (_rms(x, g1_ref[...]) * (1.0 + mod[1:2]) + mod[0:1]).astype(BF16)
    h_ext = jnp.concatenate([norm_mod(xp_ref[0]), norm_mod(x_ref[0]), norm_mod(xn_ref[0])], axis=0)
    h = h_ext[CONV_HALO:CONV_HALO + tm]
    ct, sa, sb = ct_ref[...], sa_ref[...], sb_ref[...]

    conv_in = _dot(h_ext, w1_ref[:, COL_CONV:COL_QA])
    u = conv_in[:, :CONV_WIDTH] * jax.nn.sigmoid(conv_in[:, CONV_WIDTH:])
    row = lax.broadcasted_iota(jnp.int32, (tm + 2 * CONV_HALO, 1), 0)
    inside = jnp.logical_and(jnp.logical_or(row >= CONV_HALO, i > 0),
                             jnp.logical_or(row < CONV_HALO + tm, i < pl.num_programs(1) - 1))
    ext_ref[...] = jnp.where(inside, u, 0.0)
    _conv_branch(ext_ref, sh_ref, cw_ref, cb_ref, lg_ref, lb_ref, z_ref, rows=conv_rows)

    q_a = _dot(h, w1_ref[:, COL_QA:COL_KVA])
    q = _dot(_rms(q_a, qg_ref[...]).astype(BF16), wq_ref[...])
    for hd in range(N_HEADS):
        qh = _rope(q[:, hd * HEAD_DIM:(hd + 1) * HEAD_DIM], ct, sa, sb)
        q_ref[0, hd] = (qh * QK_SCALE_LOG2E).astype(BF16)

    kv_a = _dot(h, w1_ref[:, COL_KVA:COL_KR])
    kv_n = _rms(kv_a, kvg_ref[...]).astype(BF16)
    k_rope = _rope(_dot(h, w1_ref[:, COL_KR:COL_GATE]), ct, sa, sb)
    kv = _dot(kv_n, wk_ref[...])
    lane = lax.broadcasted_iota(jnp.int32, k_rope.shape, 1)
    ones = jnp.ones((HEAD_DIM - V_HEAD_DIM, tm), F32)
    for hd in range(N_HEADS):
        kv_h = kv[:, hd * HEAD_DIM:(hd + 1) * HEAD_DIM]
        k_ref[0, hd] = jnp.where(lane < QK_NOPE_DIM, kv_h, k_rope).astype(BF16)
        vt_ref[0, hd, 0] = jnp.concatenate([kv_h.T[QK_NOPE_DIM:], ones], axis=0).astype(BF16)

    gates = _dot(h, w1_ref[:, COL_GATE:COL_END])
    gc_ref[0] = gates[:, :D_MODEL].astype(BF16)
    ga_ref[0] = gates[:, D_MODEL:].astype(BF16)


def _inproj(x, mod, g1, w1, qg, wq, kvg, wk, ct, sa, sb, conv_w, conv_b, ln_g, ln_b, *, tm, tk, conv_rows=32):
    b, s, _ = x.shape
    per_chunk = tk // tm
    nh = tm // CONV_HALO
    last_halo = s // CONV_HALO - 1
    tok = lambda width: pl.BlockSpec((1, tm, width), lambda bi, i: (bi, i, 0))
    head = pl.BlockSpec((1, N_HEADS, tm, HEAD_DIM), lambda bi, i: (bi, 0, i, 0))
    head_t = pl.BlockSpec((1, N_HEADS, 1, HEAD_DIM, tm), lambda bi, i: (bi, 0, i // per_chunk, 0, i % per_chunk))
    tab = pl.BlockSpec((tm, LANES), lambda bi, i: (i, 0))
    return pl.pallas_call(
        functools.partial(_inproj_kernel, conv_rows=conv_rows),
        grid=(b, s // tm),
        in_specs=[
            tok(D_MODEL),
            pl.BlockSpec((1, CONV_HALO, D_MODEL), lambda bi, i: (bi, jnp.maximum(i * nh - 1, 0), 0)),
            pl.BlockSpec((1, CONV_HALO, D_MODEL), lambda bi, i: (bi, jnp.minimum((i + 1) * nh, last_halo), 0)),
            pl.BlockSpec((1, N_MOD, D_MODEL), lambda bi, i: (bi, 0, 0)),
            _const_spec((1, D_MODEL)),
            _const_spec((D_MODEL, COL_END)),
            _const_spec((1, Q_LORA_RANK)),
            _const_spec((Q_LORA_RANK, N_HEADS * HEAD_DIM)),
            _const_spec((1, KV_LORA_RANK)),
            _const_spec((KV_LORA_RANK, N_HEADS * HEAD_DIM)),
            tab, tab, tab,
            _const_spec((CONV_KERNEL, CONV_WIDTH)),
            _const_spec((1, CONV_WIDTH)),
            _const_spec((1, CONV_WIDTH)),
            _const_spec((1, CONV_WIDTH)),
        ],
        out_specs=[tok(CONV_WIDTH), head, head, head_t, tok(D_MODEL), tok(D_MODEL)],
        out_shape=[
            jax.ShapeDtypeStruct((b, s, CONV_WIDTH), BF16),
            jax.ShapeDtypeStruct((b, N_HEADS, s, HEAD_DIM), BF16),
            jax.ShapeDtypeStruct((b, N_HEADS, s, HEAD_DIM), BF16),
            jax.ShapeDtypeStruct((b, N_HEADS, s // tk, HEAD_DIM, tk), BF16),
            jax.ShapeDtypeStruct((b, s, D_MODEL), BF16),
            jax.ShapeDtypeStruct((b, s, D_MODEL), BF16),
        ],
        scratch_shapes=[
            pltpu.VMEM((tm + 2 * CONV_HALO, CONV_WIDTH), F32),
            pltpu.VMEM((SUBLANES - 1, tm + 2 * CONV_HALO - SUBLANES, CONV_WIDTH), F32),
        ],
        compiler_params=pltpu.CompilerParams(
            dimension_semantics=("arbitrary", "arbitrary"), vmem_limit_bytes=VMEM_LIMIT_BYTES),
        name="in_proj",
    )(x, x, x, mod, g1, w1, qg, wq, kvg, wk, ct, sa, sb, conv_w, conv_b, ln_g, ln_b)


def _attn_kernel(q_ref, k_ref, vt_ref, o_ref, s0_ref, s1_ref, x0_ref, x1_ref):
    tq = s0_ref.shape[2]
    n_chunks, _, tk = vt_ref.shape[2:]
    n_q = q_ref.shape[2] // tq
    hps = q_ref.shape[1]
    max_of ={id(s0_ref): x0_ref, id(s1_ref): x1_ref}

    def scores(s_ref, hh, qi, c):
        kc = k_ref[0, hh, pl.ds(pl.multiple_of(c * tk, tk), tk), :]
        qt = q_ref[0, hh, pl.ds(pl.multiple_of(qi * tq, tq), tq), :]
        st = lax.dot_general(kc, qt, (((1,), (1,)), ((), ())), preferred_element_type=F32)
        s_ref[hh] = st
        max_of[id(s_ref)][hh] = jnp.max(st, axis=0, keepdims=True)

    def update(s_ref, hh, c, state):
        m, acc = state
        st = s_ref[hh]
        m_new = jnp.maximum(m, max_of[id(s_ref)][hh])
        alpha = jnp.exp2(m - m_new)
        pt = jnp.exp2(st - m_new).astype(BF16)
        return m_new, alpha * acc + _dot(vt_ref[0, hh, c], pt)

    def step(c, cur_ref, nxt_ref, state, nxt):
        new = []
        for hh in range(hps):
            if nxt is not None:
                scores(nxt_ref, hh, *nxt)
            new.append(update(cur_ref, hh, c, state[hh]))
        return tuple(new)

    def pairs(qi, c0, state, n_pairs, after=False):
        for p in range(n_pairs):
            c = c0 + 2 * p
            ends = after is not False and p == n_pairs - 1
            state = step(c, s0_ref, s1_ref, state, (qi, c + 1))
            state = step(c + 1, s1_ref, s0_ref, state, after if ends else (qi, c + 2))
        return state

    per_iter = 2 if n_chunks % 4 == 0 else 1
    n_iter = n_chunks // (2 * per_iter) - 1

    def q_tile(qi, is_last):
        state = tuple((jnp.full((1, tq), -jnp.inf, F32), jnp.zeros((HEAD_DIM, tq), F32)) for _ in range(hps))
        state = lax.fori_loop(0, n_iter, lambda j, st: pairs(qi, j * (2 * per_iter), st, per_iter), state)
        state = pairs(qi, n_iter * 2 * per_iter, state, per_iter, after=None if is_last else (qi + 1, 0))
        ot = jnp.concatenate([acc[:V_HEAD_DIM] / acc[V_HEAD_DIM:V_HEAD_DIM + 1] for _, acc in state], axis=0)
        o_ref[0, pl.ds(pl.multiple_of(qi * tq, tq), tq), :] = ot.T.astype(BF16)

    for hh in range(hps):
        scores(s0_ref, hh, 0, 0)

    unroll = 2

    def body(j, carry):
        for r in range(unroll):
            q_tile(j * unroll + r, False)
        return carry

    lax.fori_loop(0, n_q // unroll - 1, body, 0)
    for r in range(unroll):
        q_tile(n_q - unroll + r, r == unroll - 1)


def _attention(q, k, vt, *, tq):
    b, _, s, _ = q.shape
    n_chunks, _, tk = vt.shape[2:]
    hps = min(4, ATTN_ROWS_PER_STEP // s)
    return pl.pallas_call(
        _attn_kernel,
        grid=(b, N_HEADS // hps),
        in_specs=[
            pl.BlockSpec((1, hps, s, HEAD_DIM), lambda bi, hp: (bi, hp, 0, 0)),
            pl.BlockSpec((1, hps, s, HEAD_DIM), lambda bi, hp: (bi, hp, 0, 0)),
            pl.BlockSpec((1, hps, n_chunks, HEAD_DIM, tk), lambda bi, hp: (bi, hp, 0, 0, 0)),
        ],
        out_specs=pl.BlockSpec((1, s, hps * V_HEAD_DIM), lambda bi, hp: (bi, 0, hp)),
        out_shape=jax.ShapeDtypeStruct((b, s, N_HEADS * V_HEAD_DIM), BF16),
        scratch_shapes=[pltpu.VMEM((hps, tk, tq), F32), pltpu.VMEM((hps, tk, tq), F32),
                        pltpu.VMEM((hps, 1, tq), F32), pltpu.VMEM((hps, 1, tq), F32)],
        compiler_params=pltpu.CompilerParams(
            dimension_semantics=("arbitrary", "arbitrary"), vmem_limit_bytes=VMEM_LIMIT_BYTES),
        name="attention",
    )(q, k, vt)


def _outproj_kernel(x_ref, mod_ref, o_ref, z_ref, gc_ref, ga_ref, wao_ref, wco_ref, wout_ref, g2_ref,
                    wup_ref, wdn_ref, fg_ref, y_ref, *, final):
    x = x_ref[0]
    mod = mod_ref[0]
    y_attn = _dot(o_ref[0], wao_ref[...])
    y_conv = _dot(z_ref[0], wco_ref[...])
    g_conv = jax.nn.sigmoid(gc_ref[0].astype(F32))
    g_attn = jax.nn.sigmoid(ga_ref[0].astype(F32))
    mix_in = (g_conv * y_conv + g_attn * y_attn).astype(BF16)
    x1 = x + mod[2:3] * _dot(mix_in, wout_ref[...])
    h2 = (_rms(x1, g2_ref[...]) * (1.0 + mod[4:5]) + mod[3:4]).astype(BF16)
    acc = jnp.zeros(x.shape, F32)
    for c in range(0, D_FF, FF_CHUNK):
        up = _dot(h2, wup_ref[:, c:c + FF_CHUNK])
        acc = acc + _dot(jnp.square(jnp.maximum(up, 0.0)).astype(BF16), wdn_ref[c:c + FF_CHUNK, :])
    x2 = x1 + mod[5:6] * acc
    if final:
        x2 = _rms(x2, fg_ref[...])
    y_ref[0] = x2


def _outproj(x, mod, o, z, gc, ga, wao, wco, wout, g2, wup, wdn, fg, *, tm, final):
    b, s, _ = x.shape
    tok = lambda width: pl.BlockSpec((1, tm, width), lambda bi, i: (bi, i, 0))
    return pl.pallas_call(
        functools.partial(_outproj_kernel, final=final),
        grid=(b, s // tm),
        in_specs=[
            tok(D_MODEL),
            pl.BlockSpec((1, N_MOD, D_MODEL), lambda bi, i: (bi, 0, 0)),
            tok(N_HEADS * V_HEAD_DIM), tok(CONV_WIDTH), tok(D_MODEL), tok(D_MODEL),
            _const_spec((N_HEADS * V_HEAD_DIM, D_MODEL)),
            _const_spec((CONV_WIDTH, D_MODEL)),
            _const_spec((D_MODEL, D_MODEL)),
            _const_spec((1, D_MODEL)),
            _const_spec((D_MODEL, D_FF)),
            _const_spec((D_FF, D_MODEL)),
            _const_spec((1, D_MODEL)),
        ],
        out_specs=tok(D_MODEL),
        out_shape=jax.ShapeDtypeStruct((b, s, D_MODEL), F32),
        compiler_params=pltpu.CompilerParams(
            dimension_semantics=("arbitrary", "arbitrary"), vmem_limit_bytes=VMEM_LIMIT_BYTES),
        name="out_proj_mlp",
    )(x, mod, o, z, gc, ga, wao, wco, wout, g2, wup, wdn, fg)


def _rope_tables(seq_len):
    inv = ROPE_BASE ** (-jnp.arange(0, QK_ROPE_DIM, 2, dtype=F32) / QK_ROPE_DIM)
    ang = jnp.arange(seq_len, dtype=F32)[:, None] * inv[None, :]
    cos, sin = jnp.cos(ang), jnp.sin(ang)
    one = jnp.ones((seq_len, QK_NOPE_DIM), F32)
    zero = jnp.zeros((seq_len, QK_NOPE_DIM), F32)
    half = jnp.zeros_like(sin)
    ct = jnp.concatenate([one, cos, cos], axis=-1)
    sa = jnp.concatenate([zero, -sin, half], axis=-1)
    sb = jnp.concatenate([zero, half, sin], axis=-1)
    return ct, sa, sb


def _pack_w_in(w_in):
    c0 = 2 * CONV_WIDTH
    c1 = c0 + Q_LORA_RANK
    c2 = c1 + KV_LORA_RANK
    c3 = c2 + QK_ROPE_DIM
    pad = jnp.zeros((w_in.shape[0], LANES - QK_ROPE_DIM), w_in.dtype)
    return jnp.concatenate([w_in[:, :c2], pad, w_in[:, c2:c3], w_in[:, c3:]], axis=-1).astype(BF16)


def _trunk(x, mods, layer_weights, final_g, *, tm, tq, tk):
    ct, sa, sb = _rope_tables(x.shape[1])
    n_layers = len(layer_weights)
    for l, w in enumerate(layer_weights):
        mod = mods[l]
        z, q, k, vt, gc, ga = _inproj(x, mod, w["g1"], w["w1"], w["qg"], w["wq"], w["kvg"], w["wk"], ct, sa, sb,
                                      w["conv_w"], w["conv_b"], w["ln_g"], w["ln_b"], tm=tm, tk=tk)
        o = _attention(q, k, vt, tq=tq)
        x = _outproj(x, mod, o, z, gc, ga, w["wao"], w["wco"], w["wout"], w["g2"], w["wup"], w["wdn"],
                     final_g, tm=tm, final=(l == n_layers - 1))
    return x


def kernel(x_prompt, x_sample, c_prompt, c_sample, ada_w, ada_b, norm_mix_g, w_in, q_norm_g, w_q_up, kv_norm_g, w_kv_up, w_attn_o, conv_dw, conv_dw_b, conv_ln_g, conv_ln_b, w_conv_out, w_out, norm_mlp_g, w_mlp_up, w_mlp_down, final_g):
    n_layers = ada_w.shape[0]
    bp, bs = c_prompt.shape[0], c_sample.shape[0]
    pad_rows = -(bp + bs) % 8
    c_all = jnp.concatenate([c_prompt, c_sample, jnp.zeros((pad_rows, D_MODEL), F32)], axis=0)
    mods = _ada(c_all, ada_w, ada_b)
    mods_p = mods[:, :bp].reshape(n_layers, bp, N_MOD, D_MODEL)
    mods_s = mods[:, bp:bp + bs].reshape(n_layers, bs, N_MOD, D_MODEL)

    row = lambda a: a.reshape(1, -1)
    layer_weights = []
    for l in range(n_layers):
        layer_weights.append(dict(
            g1=row(norm_mix_g[l]), w1=_pack_w_in(w_in[l]), qg=row(q_norm_g[l]), wq=w_q_up[l].astype(BF16),
            kvg=row(kv_norm_g[l]), wk=w_kv_up[l].astype(BF16),
            conv_w=conv_dw[l], conv_b=row(conv_dw_b[l]), ln_g=row(conv_ln_g[l]), ln_b=row(conv_ln_b[l]),
            wao=w_attn_o[l].astype(BF16), wco=w_conv_out[l].astype(BF16), wout=w_out[l].astype(BF16),
            g2=row(norm_mlp_g[l]), wup=w_mlp_up[l].astype(BF16), wdn=w_mlp_down[l].astype(BF16)))
    fg = row(final_g)
    tiles = dict(tm=512, tk=1024, tq=256)
    y_prompt = _trunk(x_prompt, mods_p, layer_weights, fg, **tiles)
    y_sample = _trunk(x_sample, mods_s, layer_weights, fg, **tiles)
    return (y_prompt, y_sample)
```
